```python
import jax, jax.numpy as jnp
from jax import lax
import numpy as np

D_MODEL = 2048
BATCH = 4
SEQ = 4096
DEPTH = 2

HEAD_DIM = 128
MIXER_HEADS = 8
MIXER_WIDTH = MIXER_HEADS * HEAD_DIM
N_BRANCHES = 3
RET_CHUNK = 128
SB_BLOCK = 128
GDN_CHUNK = 64
CONV_WIDTH = 4
ROPE_BASE = 10000.0
D_FF = 5632
N_EXPERTS = 8
TOP_K = 2
MOE_BLOCK = 256
N_DENSE = (DEPTH + 1) // 2
N_MOE = DEPTH // 2
EPS = 1e-6

IN_SIZES = (MIXER_WIDTH, MIXER_WIDTH, MIXER_WIDTH, MIXER_WIDTH,
            MIXER_WIDTH, MIXER_WIDTH, MIXER_WIDTH,
            3 * MIXER_WIDTH, MIXER_HEADS, MIXER_HEADS, MIXER_WIDTH,
            D_MODEL, D_MODEL, D_MODEL)
IN_COLS = 11 * MIXER_WIDTH + 2 * MIXER_HEADS + N_BRANCHES * D_MODEL

kernel_name = "hybrid_retention_stickbreak_deltanet_moe"


def rms_norm(x, gain=None):
    xf = x.astype(jnp.float32)
    y = xf * lax.rsqrt(jnp.mean(xf * xf, axis=-1, keepdims=True) + EPS)
    if gain is not None:
        y = y * gain.astype(jnp.float32)
    return y.astype(x.dtype)


def l2_norm(x):
    return x * lax.rsqrt(jnp.sum(x * x, axis=-1, keepdims=True) + EPS)


def rotary(x, positions):
    half = x.shape[-1] // 2
    inv_freq = ROPE_BASE ** (-jnp.arange(half, dtype=jnp.float32) / half)
    ang = positions.astype(jnp.float32)[:, None] * inv_freq[None, :]
    cos = jnp.cos(ang)[None, :, None, :]
    sin = jnp.sin(ang)[None, :, None, :]
    x1, x2 = x[..., :half], x[..., half:]
    return jnp.concatenate([x1 * cos - x2 * sin, x2 * cos + x1 * sin], axis=-1)


def retention(q, k, v):
    b, t, h, dh = q.shape
    c = RET_CHUNK
    n = t // c
    log_gamma = jnp.log1p(-jnp.exp2(-5.0 - jnp.arange(h, dtype=jnp.float32)))
    pos = jnp.arange(c, dtype=jnp.float32)
    rel = pos[:, None] - pos[None, :]
    inner = jnp.where(rel >= 0, jnp.exp(jnp.maximum(rel, 0.0)[None] * log_gamma[:, None, None]), 0.0)
    qc = q.reshape(b, n, c, h, dh)
    kc = k.reshape(b, n, c, h, dh) * dh ** -0.5
    vc = v.reshape(b, n, c, h, dh)
    scores = jnp.einsum('bnihd,bnjhd->bnhij', qc, kc) * inner
    o_inner = jnp.einsum('bnhij,bnjhd->bnihd', scores, vc)
    k_tail = jnp.exp((c - 1.0 - pos)[:, None] * log_gamma)
    d_state = jnp.einsum('bnjhk,bnjhv->nbhkv', kc * k_tail[:, :, None], vc)
    chunk_decay = jnp.exp(c * log_gamma)[None, :, None, None]

    def step(state, ds):
        return state * chunk_decay + ds, state

    _, s_prev = lax.scan(step, jnp.zeros((b, h, dh, dh), jnp.float32), d_state)
    q_head = jnp.exp((pos + 1.0)[:, None] * log_gamma)
    o_cross = jnp.einsum('bnihk,nbhkv->bnihv', qc * q_head[:, :, None], s_prev)
    return (o_inner + o_cross).reshape(b, t, h, dh)


def stick_breaking(q, k, v):
    b, t, h, dh = q.shape
    qh = jnp.transpose(q, (0, 2, 1, 3)) * dh ** -0.5
    kh = jnp.transpose(k, (0, 2, 1, 3))
    vh = jnp.transpose(v, (0, 2, 1, 3))
    outs = []
    for blk in range(t // SB_BLOCK):
        q0 = blk * SB_BLOCK
        end = q0 + SB_BLOCK
        z = jnp.einsum('bhqd,bhkd->bhqk', qh[:, :, q0:end], kh[:, :, :end])
        tq = q0 + jnp.arange(SB_BLOCK)
        sk = jnp.arange(end)
        causal = sk[None, :] < tq[:, None]
        log_not = jnp.where(causal, jax.nn.log_sigmoid(-z), 0.0)
        tail = lax.cumsum(log_not, axis=3, reverse=True) - log_not
        weight = jnp.where(causal, jnp.exp(jax.nn.log_sigmoid(z) + tail), 0.0)
        outs.append(jnp.einsum('bhqk,bhkd->bhqd', weight, vh[:, :, :end]))
    return jnp.transpose(jnp.concatenate(outs, axis=2), (0, 2, 1, 3))


def causal_conv(x, w):
    width, ch = w.shape
    return lax.conv_general_dilated(x, w[:, None, :].astype(x.dtype), window_strides=(1,),
                                    padding=[(width - 1, 0)], dimension_numbers=('NWC', 'WIO', 'NWC'),
                                    feature_group_count=ch)


def gated_delta_rule(q, k, v, beta, g):
    b, t, h, dh = q.shape
    c = GDN_CHUNK
    n = t // c
    q = l2_norm(q) * dh ** -0.5
    k = l2_norm(k)

    def to_chunks(a):
        return a.reshape(b, n, c, h, dh).transpose(0, 3, 1, 2, 4)

    q, k, v = to_chunks(q), to_chunks(k), to_chunks(v)
    beta = beta.reshape(b, n, c, h).transpose(0, 3, 1, 2)
    g_cum = jnp.cumsum(g.reshape(b, n, c, h).transpose(0, 3, 1, 2), axis=-1)
    idx = jnp.arange(c)
    lower = idx[:, None] >= idx[None, :]
    strict = idx[:, None] > idx[None, :]
    rel = g_cum[..., :, None] - g_cum[..., None, :]
    decay = jnp.where(lower, jnp.exp(jnp.where(lower, rel, 0.0)), 0.0)
    k_beta = k * beta[..., None]
    a = jnp.where(strict, jnp.einsum('bhnid,bhnjd->bhnij', k_beta, k) * decay, 0.0)
    eye = jnp.eye(c, dtype=jnp.float32)
    rhs = jnp.concatenate([v * beta[..., None], k_beta * jnp.exp(g_cum)[..., None]], axis=-1)
    uw = lax.linalg.triangular_solve(a + eye, rhs, left_side=True, lower=True, unit_diagonal=True)
    u, w = uw[..., :dh], uw[..., dh:]
    qk = jnp.where(lower, jnp.einsum('bhnid,bhnjd->bhnij', q, k) * decay, 0.0)
    q_dec = q * jnp.exp(g_cum)[..., None]
    k_dec = k * jnp.exp(g_cum[..., -1:] - g_cum)[..., None]
    chunk_decay = jnp.exp(g_cum[..., -1])
    xs = (jnp.moveaxis(u, 2, 0), jnp.moveaxis(w, 2, 0), jnp.moveaxis(q_dec, 2, 0),
          jnp.moveaxis(k_dec, 2, 0), jnp.moveaxis(qk, 2, 0), jnp.moveaxis(chunk_decay, 2, 0))

    def step(state, inp):
        u_n, w_n, qd_n, kd_n, qk_n, cd_n = inp
        v_new = u_n - jnp.einsum('bhcd,bhdv->bhcv', w_n, state)
        o_n = jnp.einsum('bhcd,bhdv->bhcv', qd_n, state) + jnp.einsum('bhij,bhjv->bhiv', qk_n, v_new)
        state = state * cd_n[..., None, None] + jnp.einsum('bhcd,bhcv->bhdv', kd_n, v_new)
        return state, o_n

    _, o = lax.scan(step, jnp.zeros((b, h, dh, dh), jnp.float32), xs)
    return o.transpose(1, 0, 3, 2, 4).reshape(b, t, h, dh)


def mixer_block(x, norm_w, w_in, sb_qn, sb_kn, conv_w, a_log, dt_bias, gdn_on, w_branch, w_out):
    b, t, _ = x.shape
    f32 = jnp.float32
    hidden = rms_norm(x, norm_w)
    z = hidden @ w_in
    split_points = np.cumsum(IN_SIZES)[:-1].tolist()
    (rq, rk, rv, rg, sq, sk, sv, gqkv, gb, ga, gz, gate_r, gate_s, gate_g) = jnp.split(z, split_points, axis=-1)

    def heads(a):
        return a.reshape(b, t, MIXER_HEADS, HEAD_DIM)

    positions = jnp.arange(t)
    o_r = retention(rotary(heads(rq).astype(f32), positions), rotary(heads(rk).astype(f32), positions),
                    heads(rv).astype(f32))
    o_r = rms_norm(o_r) * jax.nn.silu(heads(rg).astype(f32))
    o_s = stick_breaking(rms_norm(heads(sq), sb_qn).astype(f32), rms_norm(heads(sk), sb_kn).astype(f32),
                         heads(sv).astype(f32))
    gqkv = jax.nn.silu(causal_conv(gqkv, conv_w)).astype(f32)
    gq, gk, gv = jnp.split(gqkv, 3, axis=-1)
    beta = jax.nn.sigmoid(gb.astype(f32))
    g = -jnp.exp(a_log.astype(f32)) * jax.nn.softplus(ga.astype(f32) + dt_bias.astype(f32))
    o_g = gated_delta_rule(heads(gq), heads(gk), heads(gv), beta, g)
    o_g = rms_norm(o_g, gdn_on) * jax.nn.silu(heads(gz).astype(f32))
    merged = jnp.zeros((b, t, D_MODEL), f32)
    for i, (o_b, gate_b) in enumerate(((o_r, gate_r), (o_s, gate_s), (o_g, gate_g))):
        y_b = o_b.reshape(b, t, MIXER_WIDTH).astype(x.dtype) @ w_branch[i]
        merged = merged + jax.nn.sigmoid(gate_b.astype(f32)) * y_b.astype(f32)
    return merged.astype(x.dtype) @ w_out


def swiglu(h, w_gate, w_up, w_down):
    return (jax.nn.silu(h @ w_gate) * (h @ w_up)) @ w_down


def moe_swiglu(h, w_router, w_gate, w_up, w_down):
    n_tok, dm = h.shape
    logits = jnp.dot(h.astype(jnp.float32), w_router.astype(jnp.float32))
    top_logit, top_e = lax.top_k(logits, TOP_K)
    gate = jax.nn.softmax(top_logit, axis=-1)
    n_assign = n_tok * TOP_K
    flat_e = top_e.reshape(-1)
    flat_tok = jnp.broadcast_to(jnp.arange(n_tok, dtype=jnp.int32)[:, None], (n_tok, TOP_K)).reshape(-1)
    flat_w = gate.reshape(-1)
    order = jnp.argsort(flat_e)
    e_sorted = flat_e[order]
    counts = jnp.bincount(flat_e, length=N_EXPERTS)
    padded = (counts + MOE_BLOCK - 1) // MOE_BLOCK * MOE_BLOCK
    pad_end = jnp.cumsum(padded)
    pad_start = pad_end - padded
    sort_start = jnp.cumsum(counts) - counts
    dest = pad_start[e_sorted] + jnp.arange(n_assign) - sort_start[e_sorted]
    n_blocks = -(-n_assign // MOE_BLOCK) + N_EXPERTS
    n_slots = n_blocks * MOE_BLOCK
    slot_tok = jnp.full((n_slots,), n_tok, jnp.int32).at[dest].set(flat_tok[order])
    slot_w = jnp.zeros((n_slots,), jnp.float32).at[dest].set(flat_w[order])
    block_e = jnp.minimum(jnp.searchsorted(pad_end, jnp.arange(n_blocks) * MOE_BLOCK, side='right'),
                          N_EXPERTS - 1)
    h_pad = jnp.concatenate([h, jnp.zeros((1, dm), h.dtype)], axis=0)
    xb = h_pad[slot_tok].reshape(n_blocks, MOE_BLOCK, dm)

    def expert_block(args):
        xe, e = args
        return (jax.nn.silu(xe @ w_gate[e]) * (xe @ w_up[e])) @ w_down[e]

    yb = lax.map(expert_block, (xb, block_e)).reshape(n_slots, dm)
    yb = yb * slot_w[:, None].astype(yb.dtype)
    return jnp.zeros((n_tok + 1, dm), yb.dtype).at[slot_tok].add(yb)[:n_tok]


def setup_inputs(seed: int = 0) -> dict:
    key = jax.random.key(seed)
    ks = jax.random.split(key, 20)
    f32 = jnp.float32

    def nrm(k, shape, fan_in):
        return jax.random.normal(k, shape, f32) * fan_in ** -0.5

    def gain(k, shape):
        return 1.0 + 0.02 * jax.random.normal(k, shape, f32)

    dt = jnp.exp(jax.random.uniform(ks[7], (DEPTH, MIXER_HEADS), f32) * (jnp.log(0.1) - jnp.log(0.001)) + jnp.log(0.001))
    return {
        "x": jax.random.normal(ks[0], (BATCH, SEQ, D_MODEL), f32),
        "mix_norm": gain(ks[1], (DEPTH, D_MODEL)),
        "w_in": nrm(ks[2], (DEPTH, D_MODEL, IN_COLS), D_MODEL),
        "sb_q_norm": gain(ks[3], (DEPTH, HEAD_DIM)),
        "sb_k_norm": gain(ks[4], (DEPTH, HEAD_DIM)),
        "gdn_conv": nrm(ks[5], (DEPTH, CONV_WIDTH, 3 * MIXER_WIDTH), CONV_WIDTH),
        "gdn_a_log": jnp.log(jax.random.uniform(ks[6], (DEPTH, MIXER_HEADS), f32, 1.0, 16.0)),
        "gdn_dt_bias": dt + jnp.log(-jnp.expm1(-dt)),
        "gdn_out_norm": gain(ks[8], (DEPTH, HEAD_DIM)),
        "w_branch": nrm(ks[9], (DEPTH, N_BRANCHES, MIXER_WIDTH, D_MODEL), MIXER_WIDTH),
        "w_out": nrm(ks[10], (DEPTH, D_MODEL, D_MODEL), D_MODEL),
        "ffn_norm": gain(ks[11], (DEPTH, D_MODEL)),
        "w_ffn_gate": nrm(ks[12], (N_DENSE, D_MODEL, D_FF), D_MODEL),
        "w_ffn_up": nrm(ks[13], (N_DENSE, D_MODEL, D_FF), D_MODEL),
        "w_ffn_down": nrm(ks[14], (N_DENSE, D_FF, D_MODEL), D_FF),
        "w_router": nrm(ks[15], (N_MOE, D_MODEL, N_EXPERTS), D_MODEL),
        "w_exp_gate": nrm(ks[16], (N_MOE, N_EXPERTS, D_MODEL, D_FF), D_MODEL),
        "w_exp_up": nrm(ks[17], (N_MOE, N_EXPERTS, D_MODEL, D_FF), D_MODEL),
        "w_exp_down": nrm(ks[18], (N_MOE, N_EXPERTS, D_FF, D_MODEL), D_FF),
    }


def reference(x, mix_norm, w_in, sb_q_norm, sb_k_norm, gdn_conv, gdn_a_log, gdn_dt_bias, gdn_out_norm,
              w_branch, w_out, ffn_norm, w_ffn_gate, w_ffn_up, w_ffn_down, w_router, w_exp_gate,
              w_exp_up, w_exp_down):
    for layer in range(DEPTH):
        x = x + mixer_block(x, mix_norm[layer], w_in[layer], sb_q_norm[layer], sb_k_norm[layer],
                            gdn_conv[layer], gdn_a_log[layer], gdn_dt_bias[layer], gdn_out_norm[layer],
                            w_branch[layer], w_out[layer])
        h = rms_norm(x, ffn_norm[layer])
        i = layer // 2
        if layer % 2 == 0:
            y = swiglu(h, w_ffn_gate[i], w_ffn_up[i], w_ffn_down[i])
        else:
            y = moe_swiglu(h.reshape(-1, D_MODEL), w_router[i], w_exp_gate[i], w_exp_up[i],
                           w_exp_down[i]).reshape(x.shape)
        x = x + y
    return x
```

```python
import functools

import numpy as np
import jax
import jax.numpy as jnp
from jax import lax
from jax.experimental import pallas as pl
from jax.experimental.pallas import tpu as pltpu

F32 = jnp.float32
BF16 = jnp.bfloat16

HEAD_DIM = 128
HEADS = 8
WIDTH = HEADS * HEAD_DIM
RET_CHUNK = 128
SB_BLOCK = 128
GDN_CHUNK = 64
GDN_BLOCK = 256
GDN_HEADS_PER_STEP = 4
CONV_WIDTH = 4
ROPE_BASE = 10000.0
N_EXPERTS = 8
TOP_K = 2
EPS = 1e-6
LANES = 128
SUBLANES = 8
VMEM_LIMIT = 56 * 1024 * 1024

COL_RQ, COL_RK, COL_RV, COL_RG = 0, WIDTH, 2 * WIDTH, 3 * WIDTH
COL_SQ, COL_SK, COL_SV = 4 * WIDTH, 5 * WIDTH, 6 * WIDTH
COL_GQ, COL_GK, COL_GV, COL_GZ = 7 * WIDTH, 8 * WIDTH, 9 * WIDTH, 10 * WIDTH
COL_GATES = 11 * WIDTH
SMALL_OFF = 10 * WIDTH

NT = (((1,), (1,)), ((), ()))
TN = (((0,), (0,)), ((), ()))


def _params(*sem):
    return pltpu.CompilerParams(dimension_semantics=sem, vmem_limit_bytes=VMEM_LIMIT)


def _dot(a, b):
    return jnp.dot(a.astype(BF16), b.astype(BF16), preferred_element_type=F32)


def _dot_nt(a, b):
    return lax.dot_general(a.astype(BF16), b.astype(BF16), NT, preferred_element_type=F32)


def _dot_tn(a, b):
    return lax.dot_general(a.astype(BF16), b.astype(BF16), TN, preferred_element_type=F32)


def _split3(a):
    hi = a.astype(BF16)
    r1 = a - hi.astype(F32)
    mid = r1.astype(BF16)
    lo = (r1 - mid.astype(F32)).astype(BF16)
    return hi, mid, lo


def _dot_exact_rhs(a, b01):
    return sum(jnp.dot(t, b01, preferred_element_type=F32) for t in _split3(a))


def _dot_exact_lhs(a01, b):
    return sum(jnp.dot(a01, t, preferred_element_type=F32) for t in _split3(b))


def _silu(x):
    return x * jax.nn.sigmoid(x)


def _rms(x):
    return x * lax.rsqrt(jnp.mean(x * x, axis=-1, keepdims=True) + EPS)


def _norm_small_kernel(x_ref, g_ref, ws_ref, wst_ref, h_ref, zs_ref, zst_ref):
    hb = (_rms(x_ref[...]) * g_ref[...]).astype(BF16)
    h_ref[...] = hb
    zs_ref[...] = jnp.dot(hb, ws_ref[...], preferred_element_type=F32)
    zst_ref[...] = lax.dot_general(wst_ref[...], hb, NT, preferred_element_type=F32)


def _norm_small(x2, gain, w_small, w_small_t, tm):
    n, d = x2.shape
    return pl.pallas_call(
        _norm_small_kernel,
        grid=(n // tm,),
        in_specs=[pl.BlockSpec((tm, d), lambda i: (i, 0)),
                  pl.BlockSpec((1, d), lambda i: (0, 0)),
                  pl.BlockSpec((d, LANES), lambda i: (0, 0)),
                  pl.BlockSpec((2 * HEADS, d), lambda i: (0, 0))],
        out_specs=[pl.BlockSpec((tm, d), lambda i: (i, 0)),
                   pl.BlockSpec((tm, LANES), lambda i: (i, 0)),
                   pl.BlockSpec((2 * HEADS, tm), lambda i: (0, i))],
        out_shape=[jax.ShapeDtypeStruct((n, d), BF16),
                   jax.ShapeDtypeStruct((n, LANES), F32),
                   jax.ShapeDtypeStruct((2 * HEADS, n), F32)],
        compiler_params=_params("parallel"),
        name="norm_small",
    )(x2, gain, w_small, w_small_t)


def _mm_kernel(a_ref, w_ref, o_ref):
    o_ref[...] = jnp.dot(a_ref[...], w_ref[...], preferred_element_type=F32).astype(o_ref.dtype)


def _matmul(a, w, tm, tn, out_dtype):
    m, k = a.shape
    n = w.shape[1]
    return pl.pallas_call(
        _mm_kernel,
        grid=(m // tm, n // tn),
        in_specs=[pl.BlockSpec((tm, k), lambda i, j: (i, 0)),
                  pl.BlockSpec((k, tn), lambda i, j: (0, j))],
        out_specs=pl.BlockSpec((tm, tn), lambda i, j: (i, j)),
        out_shape=jax.ShapeDtypeStruct((m, n), out_dtype),
        compiler_params=_params("parallel", "arbitrary"),
        name="matmul",
    )(a, w)


def _mm_res_kernel(a_ref, w_ref, r_ref, o_ref):
    o_ref[...] = r_ref[...] + jnp.dot(a_ref[...], w_ref[...], preferred_element_type=F32)


def _matmul_residual(a, w, res, tm, tn):
    m, k = a.shape
    n = w.shape[1]
    return pl.pallas_call(
        _mm_res_kernel,
        grid=(m // tm, n // tn),
        in_specs=[pl.BlockSpec((tm, k), lambda i, j: (i, 0)),
                  pl.BlockSpec((k, tn), lambda i, j: (0, j)),
                  pl.BlockSpec((tm, tn), lambda i, j: (i, j))],
        out_specs=pl.BlockSpec((tm, tn), lambda i, j: (i, j)),
        out_shape=jax.ShapeDtypeStruct((m, n), F32),
        compiler_params=_params("parallel", "arbitrary"),
        name="matmul_residual",
    )(a, w, res)


def _ret_kernel(q_ref, k_ref, v_ref, g_ref, cos_ref, sin_ref, inner_ref, qh_ref, kt_ref, cd_ref,
                o_ref, s_ref, *, n_chunks):
    @pl.when(pl.program_id(2) == 0)
    def _():
        s_ref[...] = jnp.zeros_like(s_ref)

    inner = inner_ref[0]
    q_head = qh_ref[0]
    k_tail = kt_ref[0]
    chunk_decay = cd_ref[0][0:1, :]
    scale = HEAD_DIM ** -0.5
    c = RET_CHUNK
    for ci in range(n_chunks):
        rows = slice(ci * c, (ci + 1) * c)
        cos, sin = cos_ref[rows, :], sin_ref[rows, :]
        q, k = q_ref[rows, :], k_ref[rows, :]
        qr = q * cos + pltpu.roll(q, HEAD_DIM // 2, 1) * sin
        kr = (k * cos + pltpu.roll(k, HEAD_DIM // 2, 1) * sin) * scale
        vb = v_ref[rows, :].astype(BF16)
        state = s_ref[...]
        scores = _dot_nt(qr, kr) * inner
        o = _dot(scores, vb) + _dot(qr * q_head, state)
        s_ref[...] = state * chunk_decay + _dot_tn(kr * k_tail, vb)
        o_ref[rows, :] = (_rms(o) * _silu(g_ref[rows, :])).astype(BF16)


def _retention_tables(t):
    h = np.arange(HEADS, dtype=np.float64)
    log_gamma = np.log1p(-np.exp2(-5.0 - h))
    pos = np.arange(RET_CHUNK, dtype=np.float64)
    rel = pos[:, None] - pos[None, :]
    inner = np.where(rel >= 0, np.exp(np.maximum(rel, 0.0)[None] * log_gamma[:, None, None]), 0.0)
    k_tail = np.exp((RET_CHUNK - 1.0 - pos)[None, :] * log_gamma[:, None])
    q_head = np.exp((pos + 1.0)[None, :] * log_gamma[:, None])
    chunk_decay = np.exp(RET_CHUNK * log_gamma)
    half = HEAD_DIM // 2
    inv_freq = ROPE_BASE ** (-np.arange(half, dtype=np.float64) / half)
    ang = np.arange(t, dtype=np.float64)[:, None] * inv_freq[None, :]
    cos = np.concatenate([np.cos(ang), np.cos(ang)], axis=1)
    sin = np.concatenate([-np.sin(ang), np.sin(ang)], axis=1)
    bc = lambda a: np.broadcast_to(a[:, :, None], (HEADS, RET_CHUNK, HEAD_DIM))
    f = lambda a: jnp.asarray(np.ascontiguousarray(a), F32)
    return (f(cos), f(sin), f(inner), f(bc(q_head)), f(bc(k_tail)),
            f(np.broadcast_to(chunk_decay[:, None, None], (HEADS, SUBLANES, HEAD_DIM))))


def _retention(z, b, t, tb):
    n = b * t
    nt = t // tb
    cos, sin, inner, q_head, k_tail, chunk_decay = _retention_tables(t)
    col = lambda off: pl.BlockSpec((tb, HEAD_DIM), lambda bi, h, i, off=off: (bi * nt + i, off // HEAD_DIM + h))
    pos = pl.BlockSpec((tb, HEAD_DIM), lambda bi, h, i: (i, 0))
    head_tab = lambda r: pl.BlockSpec((1, r, HEAD_DIM), lambda bi, h, i: (h, 0, 0))
    return pl.pallas_call(
        functools.partial(_ret_kernel, n_chunks=tb // RET_CHUNK),
        grid=(b, HEADS, nt),
        in_specs=[col(COL_RQ), col(COL_RK), col(COL_RV), col(COL_RG), pos, pos,
                  head_tab(RET_CHUNK), head_tab(RET_CHUNK), head_tab(RET_CHUNK), head_tab(SUBLANES)],
        out_specs=pl.BlockSpec((tb, HEAD_DIM), lambda bi, h, i: (bi * nt + i, h)),
        out_shape=jax.ShapeDtypeStruct((n, WIDTH), BF16),
        scratch_shapes=[pltpu.VMEM((HEAD_DIM, HEAD_DIM), F32)],
        compiler_params=_params("parallel", "parallel", "arbitrary"),
        name="retention",
    )(z, z, z, z, cos, sin, inner, q_head, k_tail, chunk_decay)


def _sb_kernel(q_ref, k_ref, v_ref, qn_ref, kn_ref, u_ref, o_ref, kb_ref, vb_ref, *, prep_rows):
    i = pl.program_id(2)
    t = k_ref.shape[0]
    bq = SB_BLOCK

    @pl.when(i == 0)
    def _():
        def prep(c, carry):
            rows = pl.ds(pl.multiple_of(c * prep_rows, prep_rows), prep_rows)
            kb_ref[rows, :] = (_rms(k_ref[rows, :]) * kn_ref[...]).astype(BF16)
            vb_ref[rows, :] = v_ref[rows, :].astype(BF16)
            return carry
        lax.fori_loop(0, t // prep_rows, prep, 0)

    qb = (_rms(q_ref[...]) * qn_ref[...] * HEAD_DIM ** -0.5).astype(BF16)
    u = u_ref[...]
    causal = lax.broadcasted_iota(jnp.int32, (bq, bq), 1) < lax.broadcasted_iota(jnp.int32, (bq, bq), 0)

    def tile(j, acc, carry, masked):
        rows = pl.ds(pl.multiple_of(j * bq, bq), bq)
        z = lax.dot_general(qb, kb_ref[rows, :], NT, preferred_element_type=F32)
        log_not = -(jnp.maximum(z, 0.0) + jnp.log1p(jnp.exp(-jnp.abs(z))))
        if masked:
            log_not = jnp.where(causal, log_not, 0.0)
        sums = _dot_exact_rhs(log_not, u)
        w = jnp.exp(log_not + z + sums[:, :bq] + carry)
        if masked:
            w = jnp.where(causal, w, 0.0)
        acc = acc + jnp.dot(w.astype(BF16), vb_ref[rows, :], preferred_element_type=F32)
        return acc, carry + sums[:, bq:]

    zero = jnp.zeros((bq, bq), F32)
    acc, carry = tile(i, zero, zero, True)

    def body(jj, ac):
        return tile(i - 1 - jj, ac[0], ac[1], False)

    acc, _ = lax.fori_loop(0, i, body, (acc, carry))
    o_ref[...] = acc.astype(BF16)


def _stick_breaking(z, qn, kn, b, t):
    n = b * t
    bq = SB_BLOCK
    nt = t // bq
    j = np.arange(bq)
    u = np.concatenate([(j[:, None] > j[None, :]), np.ones((bq, bq), bool)], axis=1)
    u = jnp.asarray(u, BF16)
    seq = lambda off: pl.BlockSpec((t, HEAD_DIM), lambda bi, h, i, off=off: (bi, off // HEAD_DIM + h))
    return pl.pallas_call(
        functools.partial(_sb_kernel, prep_rows=min(256, t)),
        grid=(b, HEADS, nt),
        in_specs=[pl.BlockSpec((bq, HEAD_DIM), lambda bi, h, i: (bi * nt + i, COL_SQ // HEAD_DIM + h)),
                  seq(COL_SK), seq(COL_SV),
                  pl.BlockSpec((1, HEAD_DIM), lambda bi, h, i: (0, 0)),
                  pl.BlockSpec((1, HEAD_DIM), lambda bi, h, i: (0, 0)),
                  pl.BlockSpec((bq, 2 * bq), lambda bi, h, i: (0, 0))],
        out_specs=pl.BlockSpec((bq, HEAD_DIM), lambda bi, h, i: (bi * nt + i, h)),
        out_shape=jax.ShapeDtypeStruct((n, WIDTH), BF16),
        scratch_shapes=[pltpu.VMEM((t, HEAD_DIM), BF16), pltpu.VMEM((t, HEAD_DIM), BF16)],
        compiler_params=_params("parallel", "parallel", "arbitrary"),
        name="stick_breaking",
    )(z, z, z, qn, kn, u)


def _unit_lower_inverse(a, eye):
    def mm(x, y):
        xs, ys = _split3(x), _split3(y)
        d = lambda p, q: jnp.dot(p, q, preferred_element_type=F32)
        return d(xs[0], ys[0]) + (d(xs[0], ys[1]) + d(xs[1], ys[0]))
    x = eye - a
    p = mm(a, a)
    x = x + mm(x, p)
    for _ in range(4):
        p = mm(p, p)
        x = x + mm(x, p)
    return x


def _gdn_kernel(xq_ref, xk_ref, xv_ref, gz_ref, cq_ref, ck_ref, cv_ref, zs_ref, gat_ref, alog_ref, dt_ref,
                alog_t_ref, dt_t_ref, rep_ref, cum_ref, cum_t_ref, gn_ref,
                o_ref, bq_ref, bk_ref, bv_ref, s_ref, *, hb):
    tb = GDN_BLOCK
    c = GDN_CHUNK
    w = hb * HEAD_DIM
    halo = SUBLANES

    @pl.when(pl.program_id(2) == 0)
    def _():
        s_ref[...] = jnp.zeros_like(s_ref)
        for buf in (bq_ref, bk_ref, bv_ref):
            buf[0:halo, :] = jnp.zeros((halo, w), F32)

    def conv(x_ref, cw_ref, buf):
        buf[halo:halo + tb, :] = x_ref[...]
        y = cw_ref[CONV_WIDTH - 1:CONV_WIDTH, :] * buf[halo:halo + tb, :]
        for tap in range(CONV_WIDTH - 1):
            back = CONV_WIDTH - 1 - tap
            y = y + cw_ref[tap:tap + 1, :] * buf[halo - back:halo - back + tb, :]
        buf[0:halo, :] = buf[tb:tb + halo, :]
        buf[halo:halo + tb, :] = _silu(y)

    conv(xq_ref, cq_ref, bq_ref)
    conv(xk_ref, ck_ref, bk_ref)
    conv(xv_ref, cv_ref, bv_ref)

    rep = _dot_exact_rhs(zs_ref[...], rep_ref[0])
    beta_all = jax.nn.sigmoid(rep[:, :w])
    g_all = -jnp.exp(alog_ref[...]) * jax.nn.softplus(rep[:, w:] + dt_ref[...])
    gc_all = _dot_exact_lhs(cum_ref[...], g_all)
    g_t = -jnp.exp(alog_t_ref[0]) * jax.nn.softplus(gat_ref[0] + dt_t_ref[0])
    gc_t = _dot_exact_rhs(g_t, cum_t_ref[...])

    ri = lax.broadcasted_iota(jnp.int32, (tb, tb), 0)
    ci = lax.broadcasted_iota(jnp.int32, (tb, tb), 1)
    same = (ri // c) == (ci // c)
    lower = same & (ri >= ci)
    strict = same & (ri > ci)
    eye = (ri == ci).astype(F32)
    scale = HEAD_DIM ** -0.5

    for hh in range(hb):
        sl = slice(hh * HEAD_DIM, (hh + 1) * HEAD_DIM)
        q = bq_ref[halo:halo + tb, sl]
        k = bk_ref[halo:halo + tb, sl]
        v = bv_ref[halo:halo + tb, sl]
        q = q * lax.rsqrt(jnp.sum(q * q, axis=-1, keepdims=True) + EPS) * scale
        k = k * lax.rsqrt(jnp.sum(k * k, axis=-1, keepdims=True) + EPS)
        beta = beta_all[:, sl]
        gc = gc_all[:, sl]
        gc_cols = jnp.concatenate([gc] * (tb // HEAD_DIM), axis=1)
        gc_rows = gc_t[hh:hh + 1, :]
        decay = jnp.where(lower, jnp.exp(jnp.where(lower, gc_cols - gc_rows, 0.0)), 0.0)
        k_beta = k * beta
        a = jnp.where(strict, _dot_nt(k_beta, k) * decay, 0.0)
        t_inv = _unit_lower_inverse(a, eye)
        e_gc = jnp.exp(gc)
        rhs = jnp.concatenate([v * beta, k_beta * e_gc], axis=1)
        uw = _dot(t_inv, rhs)
        u_all, w_all = uw[:, :HEAD_DIM], uw[:, HEAD_DIM:]
        qk = jnp.where(lower, _dot_nt(q, k) * decay, 0.0)
        q_dec = q * e_gc
        gn = gn_ref[...]
        for cc in range(tb // c):
            rows = slice(cc * c, (cc + 1) * c)
            g_last = gc[cc * c + c - 1:cc * c + c, :]
            k_dec = k[rows] * jnp.exp(g_last - gc[rows])
            state = s_ref[hh]
            v_new = u_all[rows] - _dot(w_all[rows], state)
            o = _dot(q_dec[rows], state) + _dot(qk[rows, cc * c:(cc + 1) * c], v_new)
            s_ref[hh] = state * jnp.exp(g_last) + _dot_tn(k_dec, v_new)
            o_ref[rows, sl] = (_rms(o) * gn * _silu(gz_ref[rows, sl])).astype(BF16)


def _gated_deltanet(z, zs, zst, conv_w, a_log, dt_bias, out_norm, b, t):
    n = b * t
    tb, c, hb = GDN_BLOCK, GDN_CHUNK, GDN_HEADS_PER_STEP
    nt = t // tb
    groups = HEADS // hb
    w = hb * HEAD_DIM
    rep = np.zeros((groups, LANES, 2 * w), np.float32)
    for g in range(groups):
        for hh in range(hb):
            rep[g, g * hb + hh, hh * HEAD_DIM:(hh + 1) * HEAD_DIM] = 1.0
            rep[g, HEADS + g * hb + hh, w + hh * HEAD_DIM:w + (hh + 1) * HEAD_DIM] = 1.0
    idx = np.arange(tb)
    same = (idx[:, None] // c) == (idx[None, :] // c)
    cum = (same & (idx[:, None] >= idx[None, :])).astype(np.float32)
    rep, cum, cum_t = jnp.asarray(rep, BF16), jnp.asarray(cum, BF16), jnp.asarray(cum.T, BF16)
    pad_rows = lambda a2: jnp.pad(a2.reshape(groups, hb, -1), ((0, 0), (0, SUBLANES - hb), (0, 0)))
    ga_t = pad_rows(zst[HEADS:2 * HEADS])
    alog_t = pad_rows(jnp.broadcast_to(a_log[:, None], (HEADS, tb)))
    dt_t = pad_rows(jnp.broadcast_to(dt_bias[:, None], (HEADS, tb)))
    alog = jnp.repeat(a_log, HEAD_DIM)[None, :]
    dt = jnp.repeat(dt_bias, HEAD_DIM)[None, :]
    cw = conv_w

    def col(off):
        return pl.BlockSpec((tb, w), lambda bi, g, i, off=off: (bi * nt + i, off // w + g))

    def ccol(off):
        return pl.BlockSpec((CONV_WIDTH, w), lambda bi, g, i, off=off: (0, off // w + g))

    head_row = pl.BlockSpec((1, w), lambda bi, g, i: (0, g))
    head_t = pl.BlockSpec((1, SUBLANES, tb), lambda bi, g, i: (g, 0, 0))
    const = lambda shape: pl.BlockSpec(shape, lambda bi, g, i: (0,) * len(shape))
    return pl.pallas_call(
        functools.partial(_gdn_kernel, hb=hb),
        grid=(b, groups, nt),
        in_specs=[col(COL_GQ), col(COL_GK), col(COL_GV), col(COL_GZ),
                  ccol(0), ccol(WIDTH), ccol(2 * WIDTH),
                  pl.BlockSpec((tb, LANES), lambda bi, g, i: (bi * nt + i, 0)),
                  pl.BlockSpec((1, SUBLANES, tb), lambda bi, g, i: (g, 0, bi * nt + i)),
                  head_row, head_row, head_t, head_t,
                  pl.BlockSpec((1, LANES, 2 * w), lambda bi, g, i: (g, 0, 0)),
                  const((tb, tb)), const((tb, tb)), const((1, HEAD_DIM))],
        out_specs=pl.BlockSpec((tb, w), lambda bi, g, i: (bi * nt + i, g)),
        out_shape=jax.ShapeDtypeStruct((n, WIDTH), BF16),
        scratch_shapes=[pltpu.VMEM((tb + SUBLANES, w), F32)] * 3 + [pltpu.VMEM((hb, HEAD_DIM, HEAD_DIM), F32)],
        compiler_params=_params("parallel", "parallel", "arbitrary"),
        name="gated_deltanet",
    )(z, z, z, z, cw, cw, cw, zs, ga_t, alog, dt, alog_t, dt_t, rep, cum, cum_t, out_norm)


def _merge_kernel(or_ref, os_ref, og_ref, wb_ref, gr_ref, gs_ref, gg_ref, o_ref):
    acc = jax.nn.sigmoid(gr_ref[...]) * jnp.dot(or_ref[...], wb_ref[0], preferred_element_type=F32)
    acc = acc + jax.nn.sigmoid(gs_ref[...]) * jnp.dot(os_ref[...], wb_ref[1], preferred_element_type=F32)
    acc = acc + jax.nn.sigmoid(gg_ref[...]) * jnp.dot(og_ref[...], wb_ref[2], preferred_element_type=F32)
    o_ref[...] = acc.astype(BF16)


def _merge(o_r, o_s, o_g, w_branch, z, d, tm, tn):
    n = o_r.shape[0]
    branch = pl.BlockSpec((tm, WIDTH), lambda i, j: (i, 0))
    gate = lambda bidx: pl.BlockSpec((tm, tn), lambda i, j, bidx=bidx: (i, (COL_GATES + bidx * d) // tn + j))
    return pl.pallas_call(
        _merge_kernel,
        grid=(n // tm, d // tn),
        in_specs=[branch, branch, branch,
                  pl.BlockSpec((3, WIDTH, tn), lambda i, j: (0, 0, j)),
                  gate(0), gate(1), gate(2)],
        out_specs=pl.BlockSpec((tm, tn), lambda i, j: (i, j)),
        out_shape=jax.ShapeDtypeStruct((n, d), BF16),
        compiler_params=_params("parallel", "arbitrary"),
        name="branch_merge",
    )(o_r, o_s, o_g, w_branch, z, z, z)


def _norm_kernel(x_ref, g_ref, h_ref):
    h_ref[...] = (_rms(x_ref[...]) * g_ref[...]).astype(h_ref.dtype)


def _norm(x2, gain, tm):
    n, d = x2.shape
    return pl.pallas_call(
        _norm_kernel,
        grid=(n // tm,),
        in_specs=[pl.BlockSpec((tm, d), lambda i: (i, 0)), pl.BlockSpec((1, d), lambda i: (0, 0))],
        out_specs=pl.BlockSpec((tm, d), lambda i: (i, 0)),
        out_shape=jax.ShapeDtypeStruct((n, d), BF16),
        compiler_params=_params("parallel"),
        name="norm",
    )(x2, gain)


def _norm_router_kernel(x_ref, g_ref, wr_ref, h_ref, rt_ref):
    h = _rms(x_ref[...]) * g_ref[...]
    h_ref[...] = h
    logits = jnp.dot(h, wr_ref[...], preferred_element_type=F32, precision=lax.Precision.HIGHEST)
    lane = lax.broadcasted_iota(jnp.int32, logits.shape, 1)
    neg = jnp.float32(-jnp.inf)
    lg = jnp.where(lane < N_EXPERTS, logits, neg)
    m1 = jnp.max(lg, axis=-1, keepdims=True)
    i1 = jnp.min(jnp.where(lg == m1, lane, LANES), axis=-1, keepdims=True)
    lg2 = jnp.where(lane == i1, neg, lg)
    m2 = jnp.max(lg2, axis=-1, keepdims=True)
    i2 = jnp.min(jnp.where(lg2 == m2, lane, LANES), axis=-1, keepdims=True)
    e = jnp.exp(m2 - m1)
    g1 = 1.0 / (1.0 + e)
    g2 = e / (1.0 + e)
    rt_ref[...] = jnp.where(lane == 0, i1.astype(F32),
                            jnp.where(lane == 1, i2.astype(F32),
                                      jnp.where(lane == 2, g1, jnp.where(lane == 3, g2, 0.0))))


def _norm_router(x2, gain, w_router_pad, tm):
    n, d = x2.shape
    return pl.pallas_call(
        _norm_router_kernel,
        grid=(n // tm,),
        in_specs=[pl.BlockSpec((tm, d), lambda i: (i, 0)), pl.BlockSpec((1, d), lambda i: (0, 0)),
                  pl.BlockSpec((d, LANES), lambda i: (0, 0))],
        out_specs=[pl.BlockSpec((tm, d), lambda i: (i, 0)), pl.BlockSpec((tm, LANES), lambda i: (i, 0))],
        out_shape=[jax.ShapeDtypeStruct((n, d), F32), jax.ShapeDtypeStruct((n, LANES), F32)],
        compiler_params=_params("parallel"),
        name="norm_router",
    )(x2, gain, w_router_pad)


def _swiglu_kernel(be_ref, nu_ref, x_ref, wg_ref, wu_ref, wd_ref, *rest, residual):
    if residual:
        r_ref, o_ref, xb_ref = rest
    else:
        o_ref, xb_ref = rest
    blk, f = pl.program_id(0), pl.program_id(1)

    @pl.when(blk < nu_ref[0])
    def _():
        @pl.when(f == 0)
        def _():
            xb_ref[...] = x_ref[...].astype(BF16)
            o_ref[...] = r_ref[...] if residual else jnp.zeros_like(o_ref)

        xb = xb_ref[...]
        gate = jnp.dot(xb, wg_ref[0], preferred_element_type=F32)
        up = jnp.dot(xb, wu_ref[0], preferred_element_type=F32)
        act = (_silu(gate) * up).astype(BF16)
        o_ref[...] += jnp.dot(act, wd_ref[0], preferred_element_type=F32)

    @pl.when((blk >= nu_ref[0]) & (f == 0))
    def _():
        o_ref[...] = jnp.zeros_like(o_ref)


def _grouped_swiglu(x, w_gate, w_up, w_down, block_expert, n_used, bm, tf, residual=None):
    n, d = x.shape
    d_ff = w_gate.shape[2]
    nb, nf = n // bm, d_ff // tf

    def row(blk, f, be, nu):
        return (jnp.minimum(blk, nu[0] - 1), 0)

    def f_idx(blk, f, nu):
        return jnp.where(blk < nu[0], f, nf - 1)

    def w_in_map(blk, f, be, nu):
        return (be[jnp.minimum(blk, nu[0] - 1)], 0, f_idx(blk, f, nu))

    def w_out_map(blk, f, be, nu):
        return (be[jnp.minimum(blk, nu[0] - 1)], f_idx(blk, f, nu), 0)

    in_specs = [pl.BlockSpec((bm, d), row),
                pl.BlockSpec((1, d, tf), w_in_map), pl.BlockSpec((1, d, tf), w_in_map),
                pl.BlockSpec((1, tf, d), w_out_map)]
    args = [x, w_gate, w_up, w_down]
    if residual is not None:
        in_specs.append(pl.BlockSpec((bm, d), row))
        args.append(residual)
    return pl.pallas_call(
        functools.partial(_swiglu_kernel, residual=residual is not None),
        grid_spec=pltpu.PrefetchScalarGridSpec(
            num_scalar_prefetch=2, grid=(nb, nf), in_specs=in_specs,
            out_specs=pl.BlockSpec((bm, d), lambda blk, f, be, nu: (blk, 0)),
            scratch_shapes=[pltpu.VMEM((bm, d), BF16)]),
        out_shape=jax.ShapeDtypeStruct((n, d), F32),
        compiler_params=_params("arbitrary", "arbitrary"),
        name="grouped_swiglu",
    )(block_expert, n_used, *args)


def _row_copy(src_hbm, dst_ref, src_row, dst_row, sem):
    return pltpu.make_async_copy(src_hbm.at[pl.ds(src_row, 1), :], dst_ref.at[pl.ds(dst_row, 1), :], sem)


def _gather_kernel(idx_ref, src_hbm, o_ref, sem, *, bm):
    base = pl.program_id(0) * bm

    def start(r, carry):
        _row_copy(src_hbm, o_ref, idx_ref[base + r], r, sem).start()
        return carry

    def wait(r, carry):
        _row_copy(src_hbm, o_ref, 0, r, sem).wait()
        return carry

    lax.fori_loop(0, bm, start, 0)
    lax.fori_loop(0, bm, wait, 0)


def _gather_rows(src, idx, bm):
    d = src.shape[1]
    n_out = idx.shape[0]
    return pl.pallas_call(
        functools.partial(_gather_kernel, bm=bm),
        grid_spec=pltpu.PrefetchScalarGridSpec(
            num_scalar_prefetch=1, grid=(n_out // bm,),
            in_specs=[pl.BlockSpec(memory_space=pl.ANY)],
            out_specs=pl.BlockSpec((bm, d), lambda i, idx_ref: (i, 0)),
            scratch_shapes=[pltpu.SemaphoreType.DMA(())]),
        out_shape=jax.ShapeDtypeStruct((n_out, d), src.dtype),
        compiler_params=_params("arbitrary"),
        name="gather_rows",
    )(idx, src)


def _combine_kernel(pos_ref, x_ref, rt_ref, y_hbm, o_ref, b0_ref, b1_ref, sem, *, tm):
    base = pl.program_id(0) * tm

    def start(r, carry):
        _row_copy(y_hbm, b0_ref, pos_ref[2 * (base + r)], r, sem).start()
        _row_copy(y_hbm, b1_ref, pos_ref[2 * (base + r) + 1], r, sem).start()
        return carry

    def wait(r, carry):
        _row_copy(y_hbm, b0_ref, 0, r, sem).wait()
        _row_copy(y_hbm, b1_ref, 0, r, sem).wait()
        return carry

    lax.fori_loop(0, tm, start, 0)
    lax.fori_loop(0, tm, wait, 0)
    rt = rt_ref[...]
    o_ref[...] = x_ref[...] + (b0_ref[...] * rt[:, 2:3] + b1_ref[...] * rt[:, 3:4])


def _combine(x2, route, y_sorted, pos, tm):
    n, d = x2.shape
    return pl.pallas_call(
        functools.partial(_combine_kernel, tm=tm),
        grid_spec=pltpu.PrefetchScalarGridSpec(
            num_scalar_prefetch=1, grid=(n // tm,),
            in_specs=[pl.BlockSpec((tm, d), lambda i, p: (i, 0)),
                      pl.BlockSpec((tm, LANES), lambda i, p: (i, 0)),
                      pl.BlockSpec(memory_space=pl.ANY)],
            out_specs=pl.BlockSpec((tm, d), lambda i, p: (i, 0)),
            scratch_shapes=[pltpu.VMEM((tm, d), F32), pltpu.VMEM((tm, d), F32), pltpu.SemaphoreType.DMA(())]),
        out_shape=jax.ShapeDtypeStruct((n, d), F32),
        compiler_params=_params("arbitrary"),
        name="combine",
    )(pos, x2, route, y_sorted)


def _tile(n, want):
    for cand in range(min(n, want), 0, -LANES):
        if n % cand == 0:
            return cand
    raise ValueError((n, want))


def _mixer(x2, b, t, norm_w, w_in, sb_qn, sb_kn, conv_w, a_log, dt_bias, gdn_on, w_branch, w_out):
    n, d = x2.shape
    w_main = jnp.concatenate([w_in[:, :SMALL_OFF], w_in[:, SMALL_OFF + 2 * HEADS:]], axis=1).astype(BF16)
    w_small = w_in[:, SMALL_OFF:SMALL_OFF + 2 * HEADS]
    w_small_pad = jnp.pad(w_small, ((0, 0), (0, LANES - 2 * HEADS))).astype(BF16)
    hidden, zs, zst = _norm_small(x2, norm_w[None, :], w_small_pad, w_small.T.astype(BF16), _tile(n, 512))
    z = _matmul(hidden, w_main, _tile(n, 1024), _tile(w_main.shape[1], 1024), F32)
    o_r = _retention(z, b, t, _tile(t, 512))
    o_s = _stick_breaking(z, sb_qn[None, :], sb_kn[None, :], b, t)
    o_g = _gated_deltanet(z, zs, zst, conv_w, a_log, dt_bias, gdn_on[None, :], b, t)
    merged = _merge(o_r, o_s, o_g, w_branch.astype(BF16), z, d, _tile(n, 512), _tile(d, 512))
    return _matmul_residual(merged, w_out.astype(BF16), x2, _tile(n, 1024), _tile(d, 1024))


def _dense_ffn(x2, norm_w, w_gate, w_up, w_down, bm, tf):
    n = x2.shape[0]
    h = _norm(x2, norm_w[None, :], _tile(n, 512))
    nb = n // bm
    return _grouped_swiglu(h, w_gate[None].astype(BF16), w_up[None].astype(BF16), w_down[None].astype(BF16),
                           jnp.zeros((nb,), jnp.int32), jnp.full((1,), nb, jnp.int32), bm, tf, residual=x2)


def _moe_ffn(x2, norm_w, w_router, w_gate, w_up, w_down, bm, tf):
    n, d = x2.shape
    h, route = _norm_router(x2, norm_w[None, :], jnp.pad(w_router, ((0, 0), (0, LANES - N_EXPERTS))),
                            _tile(n, 512))
    flat_e = route[:, :TOP_K].astype(jnp.int32).reshape(-1)
    n_assign = n * TOP_K
    onehot = (flat_e[:, None] == jnp.arange(N_EXPERTS, dtype=jnp.int32)[None, :]).astype(jnp.int32)
    running = jnp.cumsum(onehot, axis=0)
    counts = running[-1]
    rank = jnp.sum((running - onehot) * onehot, axis=1)
    padded = (counts + bm - 1) // bm * bm
    pad_end = jnp.cumsum(padded)
    pad_start = pad_end - padded
    dest = (pad_start[flat_e] + rank).astype(jnp.int32)
    n_blocks = -(-n_assign // bm) + N_EXPERTS
    n_slots = n_blocks * bm
    flat_tok = jnp.arange(n_assign, dtype=jnp.int32) // TOP_K
    slot_tok = jnp.zeros((n_slots,), jnp.int32).at[dest].set(flat_tok)
    block_e = jnp.minimum(jnp.searchsorted(pad_end, jnp.arange(n_blocks, dtype=jnp.int32) * bm, side='right'),
                          N_EXPERTS - 1).astype(jnp.int32)
    n_used = (pad_end[-1:] // bm).astype(jnp.int32)
    x_sorted = _gather_rows(h, slot_tok, bm)
    y_sorted = _grouped_swiglu(x_sorted, w_gate.astype(BF16), w_up.astype(BF16), w_down.astype(BF16),
                               block_e, n_used, bm, tf)
    return _combine(x2, route, y_sorted, dest, _tile(n, 256))


def kernel(x, mix_norm, w_in, sb_q_norm, sb_k_norm, gdn_conv, gdn_a_log, gdn_dt_bias, gdn_out_norm, w_branch,
           w_out, ffn_norm, w_ffn_gate, w_ffn_up, w_ffn_down, w_router, w_exp_gate, w_exp_up, w_exp_down):
    b, t, d = x.shape
    depth = mix_norm.shape[0]
    d_ff = w_ffn_gate.shape[2]
    x2 = x.reshape(b * t, d)
    bm = _tile(b * t, 512)
    tf = _tile(d_ff, 512)
    for layer in range(depth):
        x2 = _mixer(x2, b, t, mix_norm[layer], w_in[layer], sb_q_norm[layer], sb_k_norm[layer], gdn_conv[layer],
                    gdn_a_log[layer], gdn_dt_bias[layer], gdn_out_norm[layer], w_branch[layer], w_out[layer])
        i = layer // 2
        if layer % 2 == 0:
            x2 = _dense_ffn(x2, ffn_norm[layer], w_ffn_gate[i], w_ffn_up[i], w_ffn_down[i], bm, tf)
        else:
            x2 = _moe_ffn(x2, ffn_norm[layer], w_router[i], w_exp_gate[i], w_exp_up[i], w_exp_down[i], bm, tf)
    return x2.reshape(b, t, d)
```

```python
import functools

import numpy as np
import jax
import jax.numpy as jnp
from jax import lax
from jax.experimental import pallas as pl
from jax.experimental.pallas import tpu as pltpu

F32 = jnp.float32
BF16 = jnp.bfloat16

HEAD_DIM = 128
HEADS = 8
WIDTH = HEADS * HEAD_DIM
RET_CHUNK = 128
SB_TILE = 256
SB_DEAD_CARRY = -104.0
GDN_CHUNK = 64
GDN_BLOCK = 256
CONV_WIDTH = 4
ROPE_BASE = 10000.0
N_EXPERTS = 8
TOP_K = 2
EPS = 1e-6
LANES = 128
SUBLANES = 8
VMEM_LIMIT = 56 * 1024 * 1024

COL_RQ, COL_RK, COL_RV, COL_RG = 0, WIDTH, 2 * WIDTH, 3 * WIDTH
COL_SQ, COL_SK, COL_SV = 4 * WIDTH, 5 * WIDTH, 6 * WIDTH
COL_GQ, COL_GK, COL_GV, COL_GZ = 7 * WIDTH, 8 * WIDTH, 9 * WIDTH, 10 * WIDTH
COL_GATES = 11 * WIDTH
SMALL_OFF = 10 * WIDTH

NT = (((1,), (1,)), ((), ()))
TN = (((0,), (0,)), ((), ()))


def _params(*sem):
    return pltpu.CompilerParams(dimension_semantics=sem, vmem_limit_bytes=VMEM_LIMIT)


def _dot(a, b):
    return jnp.dot(a.astype(BF16), b.astype(BF16), preferred_element_type=F32)


def _dot_nt(a, b):
    return lax.dot_general(a.astype(BF16), b.astype(BF16), NT, preferred_element_type=F32)


def _dot_tn(a, b):
    return lax.dot_general(a.astype(BF16), b.astype(BF16), TN, preferred_element_type=F32)


def _split3(a):
    hi = a.astype(BF16)
    r1 = a - hi.astype(F32)
    mid = r1.astype(BF16)
    lo = (r1 - mid.astype(F32)).astype(BF16)
    return hi, mid, lo


def _dot_exact_rhs(a, b01):
    return sum(jnp.dot(t, b01, preferred_element_type=F32) for t in _split3(a))


def _dot_exact_lhs(a01, b):
    return sum(jnp.dot(a01, t, preferred_element_type=F32) for t in _split3(b))


def _silu(x):
    return x * jax.nn.sigmoid(x)


def _rms(x):
    return x * lax.rsqrt(jnp.mean(x * x, axis=-1, keepdims=True) + EPS)


def _norm_small_kernel(x_ref, g_ref, ws_ref, wst_ref, h_ref, zs_ref, zst_ref):
    hb = (_rms(x_ref[...]) * g_ref[...]).astype(BF16)
    h_ref[...] = hb
    zs_ref[...] = jnp.dot(hb, ws_ref[...], preferred_element_type=F32)
    zst_ref[...] = lax.dot_general(wst_ref[...], hb, NT, preferred_element_type=F32)


def _norm_small(x2, gain, w_small, w_small_t, tm):
    n, d = x2.shape
    return pl.pallas_call(
        _norm_small_kernel,
        grid=(n // tm,),
        in_specs=[pl.BlockSpec((tm, d), lambda i: (i, 0)),
                  pl.BlockSpec((1, d), lambda i: (0, 0)),
                  pl.BlockSpec((d, LANES), lambda i: (0, 0)),
                  pl.BlockSpec((2 * HEADS, d), lambda i: (0, 0))],
        out_specs=[pl.BlockSpec((tm, d), lambda i: (i, 0)),
                   pl.BlockSpec((tm, LANES), lambda i: (i, 0)),
                   pl.BlockSpec((2 * HEADS, tm), lambda i: (0, i))],
        out_shape=[jax.ShapeDtypeStruct((n, d), BF16),
                   jax.ShapeDtypeStruct((n, LANES), F32),
                   jax.ShapeDtypeStruct((2 * HEADS, n), F32)],
        compiler_params=_params("parallel"),
        name="norm_small",
    )(x2, gain, w_small, w_small_t)


def _mm_kernel(a_ref, w_ref, o_ref):
    o_ref[...] = jnp.dot(a_ref[...], w_ref[...], preferred_element_type=F32).astype(o_ref.dtype)


def _matmul(a, w, tm, tn, out_dtype):
    m, k = a.shape
    n = w.shape[1]
    return pl.pallas_call(
        _mm_kernel,
        grid=(m // tm, n // tn),
        in_specs=[pl.BlockSpec((tm, k), lambda i, j: (i, 0)),
                  pl.BlockSpec((k, tn), lambda i, j: (0, j))],
        out_specs=pl.BlockSpec((tm, tn), lambda i, j: (i, j)),
        out_shape=jax.ShapeDtypeStruct((m, n), out_dtype),
        compiler_params=_params("parallel", "arbitrary"),
        name="matmul",
    )(a, w)


def _mm_res_kernel(a_ref, w_ref, r_ref, o_ref):
    o_ref[...] = r_ref[...] + jnp.dot(a_ref[...], w_ref[...], preferred_element_type=F32)


def _matmul_residual(a, w, res, tm, tn):
    m, k = a.shape
    n = w.shape[1]
    return pl.pallas_call(
        _mm_res_kernel,
        grid=(m // tm, n // tn),
        in_specs=[pl.BlockSpec((tm, k), lambda i, j: (i, 0)),
                  pl.BlockSpec((k, tn), lambda i, j: (0, j)),
                  pl.BlockSpec((tm, tn), lambda i, j: (i, j))],
        out_specs=pl.BlockSpec((tm, tn), lambda i, j: (i, j)),
        out_shape=jax.ShapeDtypeStruct((m, n), F32),
        compiler_params=_params("parallel", "arbitrary"),
        name="matmul_residual",
    )(a, w, res)


def _ret_kernel(q_ref, k_ref, v_ref, g_ref, cos_ref, sin_ref, inner_ref, qh_ref, kt_ref, cd_ref,
                o_ref, s_ref, *, n_chunks):
    @pl.when(pl.program_id(2) == 0)
    def _():
        s_ref[...] = jnp.zeros_like(s_ref)

    inner = inner_ref[0]
    q_head = qh_ref[0]
    k_tail = kt_ref[0]
    chunk_decay = cd_ref[0][0:1, :]
    scale = HEAD_DIM ** -0.5
    c = RET_CHUNK
    for ci in range(n_chunks):
        rows = slice(ci * c, (ci + 1) * c)
        cos, sin = cos_ref[rows, :], sin_ref[rows, :]
        q, k = q_ref[rows, :], k_ref[rows, :]
        qr = q * cos + pltpu.roll(q, HEAD_DIM // 2, 1) * sin
        kr = (k * cos + pltpu.roll(k, HEAD_DIM // 2, 1) * sin) * scale
        vb = v_ref[rows, :].astype(BF16)
        state = s_ref[...]
        scores = _dot_nt(qr, kr) * inner
        o = _dot(scores, vb) + _dot(qr * q_head, state)
        s_ref[...] = state * chunk_decay + _dot_tn(kr * k_tail, vb)
        o_ref[rows, :] = (_rms(o) * _silu(g_ref[rows, :])).astype(BF16)


def _retention_tables(t):
    h = np.arange(HEADS, dtype=np.float64)
    log_gamma = np.log1p(-np.exp2(-5.0 - h))
    pos = np.arange(RET_CHUNK, dtype=np.float64)
    rel = pos[:, None] - pos[None, :]
    inner = np.where(rel >= 0, np.exp(np.maximum(rel, 0.0)[None] * log_gamma[:, None, None]), 0.0)
    k_tail = np.exp((RET_CHUNK - 1.0 - pos)[None, :] * log_gamma[:, None])
    q_head = np.exp((pos + 1.0)[None, :] * log_gamma[:, None])
    chunk_decay = np.exp(RET_CHUNK * log_gamma)
    half = HEAD_DIM // 2
    inv_freq = ROPE_BASE ** (-np.arange(half, dtype=np.float64) / half)
    ang = np.arange(t, dtype=np.float64)[:, None] * inv_freq[None, :]
    cos = np.concatenate([np.cos(ang), np.cos(ang)], axis=1)
    sin = np.concatenate([-np.sin(ang), np.sin(ang)], axis=1)
    bc = lambda a: np.broadcast_to(a[:, :, None], (HEADS, RET_CHUNK, HEAD_DIM))
    f = lambda a: jnp.asarray(np.ascontiguousarray(a), F32)
    return (f(cos), f(sin), f(inner), f(bc(q_head)), f(bc(k_tail)),
            f(np.broadcast_to(chunk_decay[:, None, None], (HEADS, SUBLANES, HEAD_DIM))))


def _retention(z, b, t, tb):
    n = b * t
    nt = t // tb
    cos, sin, inner, q_head, k_tail, chunk_decay = _retention_tables(t)
    col = lambda off: pl.BlockSpec((tb, HEAD_DIM), lambda bi, h, i, off=off: (bi * nt + i, off // HEAD_DIM + h))
    pos = pl.BlockSpec((tb, HEAD_DIM), lambda bi, h, i: (i, 0))
    head_tab = lambda r: pl.BlockSpec((1, r, HEAD_DIM), lambda bi, h, i: (h, 0, 0))
    return pl.pallas_call(
        functools.partial(_ret_kernel, n_chunks=tb // RET_CHUNK),
        grid=(b, HEADS, nt),
        in_specs=[col(COL_RQ), col(COL_RK), col(COL_RV), col(COL_RG), pos, pos,
                  head_tab(RET_CHUNK), head_tab(RET_CHUNK), head_tab(RET_CHUNK), head_tab(SUBLANES)],
        out_specs=pl.BlockSpec((tb, HEAD_DIM), lambda bi, h, i: (bi * nt + i, h)),
        out_shape=jax.ShapeDtypeStruct((n, WIDTH), BF16),
        scratch_shapes=[pltpu.VMEM((HEAD_DIM, HEAD_DIM), F32)],
        compiler_params=_params("parallel", "parallel", "arbitrary"),
        name="retention",
    )(z, z, z, z, cos, sin, inner, q_head, k_tail, chunk_decay)


def _split2(a):
    hi = a.astype(BF16)
    return hi, (a - hi.astype(F32)).astype(BF16)


def _sb_kernel(q_ref, k_ref, v_ref, qn_ref, kn_ref, u_ref, o_ref, kb_ref, vb_ref, acc_ref, carry_ref, *, prep_rows):
    i = pl.program_id(2)
    t = k_ref.shape[0]
    bq = q_ref.shape[0]

    @pl.when(i == 0)
    def _():
        def prep(c, carry):
            rows = pl.ds(pl.multiple_of(c * prep_rows, prep_rows), prep_rows)
            kb_ref[rows, :] = (_rms(k_ref[rows, :]) * kn_ref[...]).astype(BF16)
            vb_ref[rows, :] = v_ref[rows, :].astype(BF16)
            return carry
        lax.fori_loop(0, t // prep_rows, prep, 0)

    qb = (_rms(q_ref[...]) * qn_ref[...] * HEAD_DIM ** -0.5).astype(BF16)
    u = u_ref[...]

    def tile(j, diagonal):
        rows = pl.ds(pl.multiple_of(j * bq, bq), bq)
        z = lax.dot_general(qb, kb_ref[rows, :], NT, preferred_element_type=F32)
        log_not = -(jnp.maximum(z, 0.0) + jnp.log1p(jnp.exp(-jnp.abs(z))))
        if diagonal:
            causal = lax.broadcasted_iota(jnp.int32, (bq, bq), 1) < lax.broadcasted_iota(jnp.int32, (bq, bq), 0)
            log_not = jnp.where(causal, log_not, 0.0)
        hi, lo = _split2(log_not)
        later = jnp.dot(hi, u, preferred_element_type=F32) + jnp.dot(lo, u, preferred_element_type=F32)
        total = jnp.sum(log_not, axis=1, keepdims=True)
        if diagonal:
            w = jnp.where(causal, jnp.exp(log_not + z + later), 0.0)
            acc_ref[...] = jnp.dot(w.astype(BF16), vb_ref[rows, :], preferred_element_type=F32)
            carry = total
        else:
            carry = carry_ref[...]
            w = jnp.exp(log_not + z + later + carry)
            acc_ref[...] += jnp.dot(w.astype(BF16), vb_ref[rows, :], preferred_element_type=F32)
            carry = carry + total
        carry_ref[...] = carry
        return jnp.max(carry)

    def live(state):
        return (state[0] >= 0) & (state[1] > SB_DEAD_CARRY)

    def step(state):
        return state[0] - 1, tile(state[0], False)

    lax.while_loop(live, step, (i - 1, tile(i, True)))
    o_ref[...] = acc_ref[...].astype(BF16)


def _stick_breaking(z, qn, kn, b, t):
    n = b * t
    bq = _tile(t, SB_TILE)
    nt = t // bq
    j = np.arange(bq)
    u = jnp.asarray(j[:, None] > j[None, :], BF16)
    seq = lambda off: pl.BlockSpec((t, HEAD_DIM), lambda bi, h, i, off=off: (bi, off // HEAD_DIM + h))
    return pl.pallas_call(
        functools.partial(_sb_kernel, prep_rows=bq),
        grid=(b, HEADS, nt),
        in_specs=[pl.BlockSpec((bq, HEAD_DIM), lambda bi, h, i: (bi * nt + i, COL_SQ // HEAD_DIM + h)),
                  seq(COL_SK), seq(COL_SV),
                  pl.BlockSpec((1, HEAD_DIM), lambda bi, h, i: (0, 0)),
                  pl.BlockSpec((1, HEAD_DIM), lambda bi, h, i: (0, 0)),
                  pl.BlockSpec((bq, bq), lambda bi, h, i: (0, 0))],
        out_specs=pl.BlockSpec((bq, HEAD_DIM), lambda bi, h, i: (bi * nt + i, h)),
        out_shape=jax.ShapeDtypeStruct((n, WIDTH), BF16),
        scratch_shapes=[pltpu.VMEM((t, HEAD_DIM), BF16), pltpu.VMEM((t, HEAD_DIM), BF16),
                        pltpu.VMEM((bq, HEAD_DIM), F32), pltpu.VMEM((bq, 1), F32)],
        compiler_params=_params("parallel", "parallel", "arbitrary"),
        name="stick_breaking",
    )(z, z, z, qn, kn, u)


def _unit_lower_inverse(a, eye):
    x = eye - a
    p = _dot(a, a)
    x = x + _dot(x, p)
    for _ in range(4):
        p = _dot(p, p)
        x = x + _dot(x, p)
    return x


def _gdn_kernel(xq_ref, xk_ref, xv_ref, gz_ref, cq_ref, ck_ref, cv_ref, zs_ref, gat_ref, alog_ref, dt_ref,
                alog_t_ref, dt_t_ref, cum_ref, cum_t_ref, gn_ref, o_ref, bq_ref, bk_ref, bv_ref, s_ref):
    tb = GDN_BLOCK
    c = GDN_CHUNK
    halo = SUBLANES

    @pl.when(pl.program_id(1) == 0)
    def _():
        s_ref[...] = jnp.zeros_like(s_ref)
        for buf in (bq_ref, bk_ref, bv_ref):
            buf[0:halo, :] = jnp.zeros((halo, WIDTH), F32)

    def conv(x_ref, cw_ref, buf):
        buf[halo:halo + tb, :] = x_ref[...]
        y = cw_ref[CONV_WIDTH - 1:CONV_WIDTH, :] * buf[halo:halo + tb, :]
        for tap in range(CONV_WIDTH - 1):
            back = CONV_WIDTH - 1 - tap
            y = y + cw_ref[tap:tap + 1, :] * buf[halo - back:halo - back + tb, :]
        buf[0:halo, :] = buf[tb:tb + halo, :]
        buf[halo:halo + tb, :] = _silu(y)

    conv(xq_ref, cq_ref, bq_ref)
    conv(xk_ref, ck_ref, bk_ref)
    conv(xv_ref, cv_ref, bv_ref)

    zs = zs_ref[...]
    beta_lanes = jax.nn.sigmoid(zs)
    g_lanes = -jnp.exp(alog_ref[...]) * jax.nn.softplus(zs + dt_ref[...])
    gc_lanes = _dot_exact_lhs(cum_ref[...], g_lanes)
    g_t = -jnp.exp(alog_t_ref[...]) * jax.nn.softplus(gat_ref[...] + dt_t_ref[...])
    gc_t = _dot_exact_rhs(g_t, cum_t_ref[...])

    ri = lax.broadcasted_iota(jnp.int32, (tb, tb), 0)
    ci = lax.broadcasted_iota(jnp.int32, (tb, tb), 1)
    same = (ri // c) == (ci // c)
    lower = same & (ri >= ci)
    strict = same & (ri > ci)
    eye = (ri == ci).astype(F32)
    scale = HEAD_DIM ** -0.5
    gn = gn_ref[...]

    for hh in range(HEADS):
        sl = slice(hh * HEAD_DIM, (hh + 1) * HEAD_DIM)
        q = bq_ref[halo:halo + tb, sl]
        k = bk_ref[halo:halo + tb, sl]
        v = bv_ref[halo:halo + tb, sl]
        q = q * lax.rsqrt(jnp.sum(q * q, axis=-1, keepdims=True) + EPS) * scale
        k = k * lax.rsqrt(jnp.sum(k * k, axis=-1, keepdims=True) + EPS)
        beta = beta_lanes[:, hh:hh + 1]
        gc = gc_lanes[:, HEADS + hh:HEADS + hh + 1]
        gc_rows = gc_t[hh:hh + 1, :]
        decay = jnp.where(lower, jnp.exp(jnp.where(lower, gc - gc_rows, 0.0)), 0.0)
        k_beta = k * beta
        a = jnp.where(strict, _dot_nt(k_beta, k) * decay, 0.0)
        t_inv = _unit_lower_inverse(a, eye)
        e_gc = jnp.exp(gc)
        rhs = jnp.concatenate([v * beta, k_beta * e_gc], axis=1)
        uw = _dot(t_inv, rhs)
        u_all, w_all = uw[:, :HEAD_DIM], uw[:, HEAD_DIM:]
        qk = jnp.where(lower, _dot_nt(q, k) * decay, 0.0)
        q_dec = q * e_gc
        for cc in range(tb // c):
            rows = slice(cc * c, (cc + 1) * c)
            g_last = gc[cc * c + c - 1:cc * c + c, :]
            k_dec = k[rows] * jnp.exp(g_last - gc[rows])
            state = s_ref[hh]
            v_new = u_all[rows] - _dot(w_all[rows], state)
            o = _dot(q_dec[rows], state) + _dot(qk[rows, cc * c:(cc + 1) * c], v_new)
            s_ref[hh] = state * jnp.exp(g_last) + _dot_tn(k_dec, v_new)
            o_ref[rows, sl] = (_rms(o) * gn * _silu(gz_ref[rows, sl])).astype(BF16)


def _gated_deltanet(z, zs, zst, conv_w, a_log, dt_bias, out_norm, b, t):
    n = b * t
    tb, c = GDN_BLOCK, GDN_CHUNK
    nt = t // tb
    idx = np.arange(tb)
    same = (idx[:, None] // c) == (idx[None, :] // c)
    cum = (same & (idx[:, None] >= idx[None, :])).astype(np.float32)
    cum, cum_t = jnp.asarray(cum, BF16), jnp.asarray(cum.T, BF16)
    lanes = lambda a1: jnp.pad(a1, (HEADS, LANES - 2 * HEADS))[None, :]
    rows = lambda a1: jnp.broadcast_to(a1[:, None], (HEADS, tb))

    def col(off):
        return pl.BlockSpec((tb, WIDTH), lambda bi, i, off=off: (bi * nt + i, off // WIDTH))

    def ccol(blk):
        return pl.BlockSpec((CONV_WIDTH, WIDTH), lambda bi, i, blk=blk: (0, blk))

    const = lambda shape: pl.BlockSpec(shape, lambda bi, i: (0,) * len(shape))
    return pl.pallas_call(
        _gdn_kernel,
        grid=(b, nt),
        in_specs=[col(COL_GQ), col(COL_GK), col(COL_GV), col(COL_GZ), ccol(0), ccol(1), ccol(2),
                  pl.BlockSpec((tb, LANES), lambda bi, i: (bi * nt + i, 0)),
                  pl.BlockSpec((HEADS, tb), lambda bi, i: (1, bi * nt + i)),
                  const((1, LANES)), const((1, LANES)), const((HEADS, tb)), const((HEADS, tb)),
                  const((tb, tb)), const((tb, tb)), const((1, HEAD_DIM))],
        out_specs=pl.BlockSpec((tb, WIDTH), lambda bi, i: (bi * nt + i, 0)),
        out_shape=jax.ShapeDtypeStruct((n, WIDTH), BF16),
        scratch_shapes=[pltpu.VMEM((tb + SUBLANES, WIDTH), F32)] * 3 + [pltpu.VMEM((HEADS, HEAD_DIM, HEAD_DIM), F32)],
        compiler_params=_params("parallel", "arbitrary"),
        name="gated_deltanet",
    )(z, z, z, z, conv_w, conv_w, conv_w, zs, zst, lanes(a_log), lanes(dt_bias), rows(a_log), rows(dt_bias),
      cum, cum_t, out_norm)


def _merge_kernel(or_ref, os_ref, og_ref, wb_ref, gr_ref, gs_ref, gg_ref, o_ref):
    acc = jax.nn.sigmoid(gr_ref[...]) * jnp.dot(or_ref[...], wb_ref[0], preferred_element_type=F32)
    acc = acc + jax.nn.sigmoid(gs_ref[...]) * jnp.dot(os_ref[...], wb_ref[1], preferred_element_type=F32)
    acc = acc + jax.nn.sigmoid(gg_ref[...]) * jnp.dot(og_ref[...], wb_ref[2], preferred_element_type=F32)
    o_ref[...] = acc.astype(BF16)


def _merge(o_r, o_s, o_g, w_branch, z, d, tm, tn):
    n = o_r.shape[0]
    branch = pl.BlockSpec((tm, WIDTH), lambda i, j: (i, 0))
    gate = lambda bidx: pl.BlockSpec((tm, tn), lambda i, j, bidx=bidx: (i, (COL_GATES + bidx * d) // tn + j))
    return pl.pallas_call(
        _merge_kernel,
        grid=(n // tm, d // tn),
        in_specs=[branch, branch, branch,
                  pl.BlockSpec((3, WIDTH, tn), lambda i, j: (0, 0, j)),
                  gate(0), gate(1), gate(2)],
        out_specs=pl.BlockSpec((tm, tn), lambda i, j: (i, j)),
        out_shape=jax.ShapeDtypeStruct((n, d), BF16),
        compiler_params=_params("parallel", "arbitrary"),
        name="branch_merge",
    )(o_r, o_s, o_g, w_branch, z, z, z)


def _norm_kernel(x_ref, g_ref, h_ref):
    h_ref[...] = (_rms(x_ref[...]) * g_ref[...]).astype(h_ref.dtype)


def _norm(x2, gain, tm):
    n, d = x2.shape
    return pl.pallas_call(
        _norm_kernel,
        grid=(n // tm,),
        in_specs=[pl.BlockSpec((tm, d), lambda i: (i, 0)), pl.BlockSpec((1, d), lambda i: (0, 0))],
        out_specs=pl.BlockSpec((tm, d), lambda i: (i, 0)),
        out_shape=jax.ShapeDtypeStruct((n, d), BF16),
        compiler_params=_params("parallel"),
        name="norm",
    )(x2, gain)


def _norm_router_kernel(x_ref, g_ref, wr_ref, h_ref, rt_ref):
    h = _rms(x_ref[...]) * g_ref[...]
    h_ref[...] = h
    logits = jnp.dot(h, wr_ref[...], preferred_element_type=F32, precision=lax.Precision.HIGHEST)
    lane = lax.broadcasted_iota(jnp.int32, logits.shape, 1)
    neg = jnp.float32(-jnp.inf)
    lg = jnp.where(lane < N_EXPERTS, logits, neg)
    m1 = jnp.max(lg, axis=-1, keepdims=True)
    i1 = jnp.min(jnp.where(lg == m1, lane, LANES), axis=-1, keepdims=True)
    lg2 = jnp.where(lane == i1, neg, lg)
    m2 = jnp.max(lg2, axis=-1, keepdims=True)
    i2 = jnp.min(jnp.where(lg2 == m2, lane, LANES), axis=-1, keepdims=True)
    e = jnp.exp(m2 - m1)
    g1 = 1.0 / (1.0 + e)
    g2 = e / (1.0 + e)
    rt_ref[...] = jnp.where(lane == 0, i1.astype(F32),
                            jnp.where(lane == 1, i2.astype(F32),
                                      jnp.where(lane == 2, g1, jnp.where(lane == 3, g2, 0.0))))


def _norm_router(x2, gain, w_router_pad, tm):
    n, d = x2.shape
    return pl.pallas_call(
        _norm_router_kernel,
        grid=(n // tm,),
        in_specs=[pl.BlockSpec((tm, d), lambda i: (i, 0)), pl.BlockSpec((1, d), lambda i: (0, 0)),
                  pl.BlockSpec((d, LANES), lambda i: (0, 0))],
        out_specs=[pl.BlockSpec((tm, d), lambda i: (i, 0)), pl.BlockSpec((tm, LANES), lambda i: (i, 0))],
        out_shape=[jax.ShapeDtypeStruct((n, d), F32), jax.ShapeDtypeStruct((n, LANES), F32)],
        compiler_params=_params("parallel"),
        name="norm_router",
    )(x2, gain, w_router_pad)


def _swiglu_kernel(be_ref, nu_ref, x_ref, wg_ref, wu_ref, wd_ref, *rest, residual):
    if residual:
        r_ref, o_ref, xb_ref = rest
    else:
        o_ref, xb_ref = rest
    blk, f = pl.program_id(0), pl.program_id(1)

    @pl.when(blk < nu_ref[0])
    def _():
        @pl.when(f == 0)
        def _():
            xb_ref[...] = x_ref[...].astype(BF16)
            o_ref[...] = r_ref[...] if residual else jnp.zeros_like(o_ref)

        xb = xb_ref[...]
        gate = jnp.dot(xb, wg_ref[0], preferred_element_type=F32)
        up = jnp.dot(xb, wu_ref[0], preferred_element_type=F32)
        act = (_silu(gate) * up).astype(BF16)
        o_ref[...] += jnp.dot(act, wd_ref[0], preferred_element_type=F32)

    @pl.when((blk >= nu_ref[0]) & (f == 0))
    def _():
        o_ref[...] = jnp.zeros_like(o_ref)


def _grouped_swiglu(x, w_gate, w_up, w_down, block_expert, n_used, bm, tf, residual=None):
    n, d = x.shape
    d_ff = w_gate.shape[2]
    nb, nf = n // bm, d_ff // tf

    def row(blk, f, be, nu):
        return (jnp.minimum(blk, nu[0] - 1), 0)

    def f_idx(blk, f, nu):
        return jnp.where(blk < nu[0], f, nf - 1)

    def w_in_map(blk, f, be, nu):
        return (be[jnp.minimum(blk, nu[0] - 1)], 0, f_idx(blk, f, nu))

    def w_out_map(blk, f, be, nu):
        return (be[jnp.minimum(blk, nu[0] - 1)], f_idx(blk, f, nu), 0)

    in_specs = [pl.BlockSpec((bm, d), row),
                pl.BlockSpec((1, d, tf), w_in_map), pl.BlockSpec((1, d, tf), w_in_map),
                pl.BlockSpec((1, tf, d), w_out_map)]
    args = [x, w_gate, w_up, w_down]
    if residual is not None:
        in_specs.append(pl.BlockSpec((bm, d), row))
        args.append(residual)
    return pl.pallas_call(
        functools.partial(_swiglu_kernel, residual=residual is not None),
        grid_spec=pltpu.PrefetchScalarGridSpec(
            num_scalar_prefetch=2, grid=(nb, nf), in_specs=in_specs,
            out_specs=pl.BlockSpec((bm, d), lambda blk, f, be, nu: (blk, 0)),
            scratch_shapes=[pltpu.VMEM((bm, d), BF16)]),
        out_shape=jax.ShapeDtypeStruct((n, d), F32),
        compiler_params=_params("arbitrary", "arbitrary"),
        name="grouped_swiglu",
    )(block_expert, n_used, *args)


def _row_copy(src_hbm, dst_ref, src_row, dst_row, sem):
    return pltpu.make_async_copy(src_hbm.at[pl.ds(src_row, 1), :], dst_ref.at[pl.ds(dst_row, 1), :], sem)


def _gather_kernel(idx_ref, src_hbm, o_ref, sem, *, bm):
    base = pl.program_id(0) * bm

    def start(r, carry):
        _row_copy(src_hbm, o_ref, idx_ref[base + r], r, sem).start()
        return carry

    def wait(r, carry):
        _row_copy(src_hbm, o_ref, 0, r, sem).wait()
        return carry

    lax.fori_loop(0, bm, start, 0)
    lax.fori_loop(0, bm, wait, 0)


def _gather_rows(src, idx, bm):
    d = src.shape[1]
    n_out = idx.shape[0]
    return pl.pallas_call(
        functools.partial(_gather_kernel, bm=bm),
        grid_spec=pltpu.PrefetchScalarGridSpec(
            num_scalar_prefetch=1, grid=(n_out // bm,),
            in_specs=[pl.BlockSpec(memory_space=pl.ANY)],
            out_specs=pl.BlockSpec((bm, d), lambda i, idx_ref: (i, 0)),
            scratch_shapes=[pltpu.SemaphoreType.DMA(())]),
        out_shape=jax.ShapeDtypeStruct((n_out, d), src.dtype),
        compiler_params=_params("arbitrary"),
        name="gather_rows",
    )(idx, src)


def _combine_kernel(pos_ref, x_ref, rt_ref, y_hbm, o_ref, b0_ref, b1_ref, sem, *, tm):
    base = pl.program_id(0) * tm

    def start(r, carry):
        _row_copy(y_hbm, b0_ref, pos_ref[2 * (base + r)], r, sem).start()
        _row_copy(y_hbm, b1_ref, pos_ref[2 * (base + r) + 1], r, sem).start()
        return carry

    def wait(r, carry):
        _row_copy(y_hbm, b0_ref, 0, r, sem).wait()
        _row_copy(y_hbm, b1_ref, 0, r, sem).wait()
        return carry

    lax.fori_loop(0, tm, start, 0)
    lax.fori_loop(0, tm, wait, 0)
    rt = rt_ref[...]
    o_ref[...] = x_ref[...] + (b0_ref[...] * rt[:, 2:3] + b1_ref[...] * rt[:, 3:4])


def _combine(x2, route, y_sorted, pos, tm):
    n, d = x2.shape
    return pl.pallas_call(
        functools.partial(_combine_kernel, tm=tm),
        grid_spec=pltpu.PrefetchScalarGridSpec(
            num_scalar_prefetch=1, grid=(n // tm,),
            in_specs=[pl.BlockSpec((tm, d), lambda i, p: (i, 0)),
                      pl.BlockSpec((tm, LANES), lambda i, p: (i, 0)),
                      pl.BlockSpec(memory_space=pl.ANY)],
            out_specs=pl.BlockSpec((tm, d), lambda i, p: (i, 0)),
            scratch_shapes=[pltpu.VMEM((tm, d), F32), pltpu.VMEM((tm, d), F32), pltpu.SemaphoreType.DMA(())]),
        out_shape=jax.ShapeDtypeStruct((n, d), F32),
        compiler_params=_params("arbitrary"),
        name="combine",
    )(pos, x2, route, y_sorted)


def _tile(n, want):
    for cand in range(min(n, want), 0, -LANES):
        if n % cand == 0:
            return cand
    raise ValueError((n, want))


def _mixer(x2, b, t, norm_w, w_in, sb_qn, sb_kn, conv_w, a_log, dt_bias, gdn_on, w_branch, w_out):
    n, d = x2.shape
    w_main = jnp.concatenate([w_in[:, :SMALL_OFF], w_in[:, SMALL_OFF + 2 * HEADS:]], axis=1).astype(BF16)
    w_small = w_in[:, SMALL_OFF:SMALL_OFF + 2 * HEADS]
    w_small_pad = jnp.pad(w_small, ((0, 0), (0, LANES - 2 * HEADS))).astype(BF16)
    hidden, zs, zst = _norm_small(x2, norm_w[None, :], w_small_pad, w_small.T.astype(BF16), _tile(n, 512))
    z = _matmul(hidden, w_main, _tile(n, 1024), _tile(w_main.shape[1], 1024), F32)
    o_r = _retention(z, b, t, _tile(t, 512))
    o_s = _stick_breaking(z, sb_qn[None, :], sb_kn[None, :], b, t)
    o_g = _gated_deltanet(z, zs, zst, conv_w, a_log, dt_bias, gdn_on[None, :], b, t)
    merged = _merge(o_r, o_s, o_g, w_branch.astype(BF16), z, d, _tile(n, 512), _tile(d, 512))
    return _matmul_residual(merged, w_out.astype(BF16), x2, _tile(n, 1024), _tile(d, 1024))


def _dense_ffn(x2, norm_w, w_gate, w_up, w_down, bm, tf):
    n = x2.shape[0]
    h = _norm(x2, norm_w[None, :], _tile(n, 512))
    nb = n // bm
    return _grouped_swiglu(h, w_gate[None].astype(BF16), w_up[None].astype(BF16), w_down[None].astype(BF16),
                           jnp.zeros((nb,), jnp.int32), jnp.full((1,), nb, jnp.int32), bm, tf, residual=x2)


def _moe_ffn(x2, norm_w, w_router, w_gate, w_up, w_down, bm, tf):
    n, d = x2.shape
    h, route = _norm_router(x2, norm_w[None, :], jnp.pad(w_router, ((0, 0), (0, LANES - N_EXPERTS))),
                            _tile(n, 512))
    flat_e = route[:, :TOP_K].astype(jnp.int32).reshape(-1)
    n_assign = n * TOP_K
    onehot = (flat_e[:, None] == jnp.arange(N_EXPERTS, dtype=jnp.int32)[None, :]).astype(jnp.int32)
    running = jnp.cumsum(onehot, axis=0)
    counts = running[-1]
    rank = jnp.sum((running - onehot) * onehot, axis=1)
    padded = (counts + bm - 1) // bm * bm
    pad_end = jnp.cumsum(padded)
    pad_start = pad_end - padded
    dest = (pad_start[flat_e] + rank).astype(jnp.int32)
    n_blocks = -(-n_assign // bm) + N_EXPERTS
    n_slots = n_blocks * bm
    flat_tok = jnp.arange(n_assign, dtype=jnp.int32) // TOP_K
    slot_tok = jnp.zeros((n_slots,), jnp.int32).at[dest].set(flat_tok)
    block_e = jnp.minimum(jnp.searchsorted(pad_end, jnp.arange(n_blocks, dtype=jnp.int32) * bm, side='right'),
                          N_EXPERTS - 1).astype(jnp.int32)
    n_used = (pad_end[-1:] // bm).astype(jnp.int32)
    x_sorted = _gather_rows(h, slot_tok, bm)
    y_sorted = _grouped_swiglu(x_sorted, w_gate.astype(BF16), w_up.astype(BF16), w_down.astype(BF16),
                               block_e, n_used, bm, tf)
    return _combine(x2, route, y_sorted, dest, _tile(n, 256))


def kernel(x, mix_norm, w_in, sb_q_norm, sb_k_norm, gdn_conv, gdn_a_log, gdn_dt_bias, gdn_out_norm, w_branch,
           w_out, ffn_norm, w_ffn_gate, w_ffn_up, w_ffn_down, w_router, w_exp_gate, w_exp_up, w_exp_down):
    b, t, d = x.shape
    depth = mix_norm.shape[0]
    d_ff = w_ffn_gate.shape[2]
    x2 = x.reshape(b * t, d)
    bm = _tile(b * t, 512)
    tf = _tile(d_ff, 512)
    for layer in range(depth):
        x2 = _mixer(x2, b, t, mix_norm[layer], w_in[layer], sb_q_norm[layer], sb_k_norm[layer], gdn_conv[layer],
                    gdn_a_log[layer], gdn_dt_bias[layer], gdn_out_norm[layer], w_branch[layer], w_out[layer])
        i = layer // 2
        if layer % 2 == 0:
            x2 = _dense_ffn(x2, ffn_norm[layer], w_ffn_gate[i], w_ffn_up[i], w_ffn_down[i], bm, tf)
        else:
            x2 = _moe_ffn(x2, ffn_norm[layer], w_router[i], w_exp_gate[i], w_exp_up[i], w_exp_down[i], bm, tf)
    return x2.reshape(b, t, d)
```

```python
import functools

import numpy as np
import jax
import jax.numpy as jnp
from jax import lax
from jax.experimental import pallas as pl
from jax.experimental.pallas import tpu as pltpu

F32 = jnp.float32
BF16 = jnp.bfloat16

HEAD_DIM = 128
HEADS = 8
WIDTH = HEADS * HEAD_DIM
RET_CHUNK = 128
RET_HEAD_GROUP = 4
SB_TILE = 256
SB_STREAMS = 2
SB_DEAD_CARRY = -104.0
GDN_CHUNK = 64
GDN_BLOCK = 256
GDN_HEAD_GROUP = 4
CONV_WIDTH = 4
ROPE_BASE = 10000.0
N_EXPERTS = 8
TOP_K = 2
EPS = 1e-6
LANES = 128
SUBLANES = 8
VMEM_LIMIT = 56 * 1024 * 1024

COL_RQ, COL_RK, COL_RV, COL_RG = 0, WIDTH, 2 * WIDTH, 3 * WIDTH
COL_SQ, COL_SK, COL_SV = 4 * WIDTH, 5 * WIDTH, 6 * WIDTH
COL_GQ, COL_GK, COL_GV, COL_GZ = 7 * WIDTH, 8 * WIDTH, 9 * WIDTH, 10 * WIDTH
COL_GATES = 11 * WIDTH
SMALL_OFF = 10 * WIDTH

NT = (((1,), (1,)), ((), ()))
TN = (((0,), (0,)), ((), ()))


def _params(*sem):
    return pltpu.CompilerParams(dimension_semantics=sem, vmem_limit_bytes=VMEM_LIMIT)


def _dot(a, b):
    return jnp.dot(a.astype(BF16), b.astype(BF16), preferred_element_type=F32)


def _dot_nt(a, b):
    return lax.dot_general(a.astype(BF16), b.astype(BF16), NT, preferred_element_type=F32)


def _dot_tn(a, b):
    return lax.dot_general(a.astype(BF16), b.astype(BF16), TN, preferred_element_type=F32)


def _split3(a):
    hi = a.astype(BF16)
    r1 = a - hi.astype(F32)
    mid = r1.astype(BF16)
    lo = (r1 - mid.astype(F32)).astype(BF16)
    return hi, mid, lo


def _dot_exact_rhs(a, b01):
    return sum(jnp.dot(t, b01, preferred_element_type=F32) for t in _split3(a))


def _dot_exact_lhs(a01, b):
    return sum(jnp.dot(a01, t, preferred_element_type=F32) for t in _split3(b))


def _silu(x):
    return x * jax.nn.sigmoid(x)


def _rms(x):
    return x * lax.rsqrt(jnp.mean(x * x, axis=-1, keepdims=True) + EPS)


def _norm_small_kernel(x_ref, g_ref, ws_ref, wst_ref, h_ref, zs_ref, zst_ref):
    hb = (_rms(x_ref[...]) * g_ref[...]).astype(BF16)
    h_ref[...] = hb
    zs_ref[...] = jnp.dot(hb, ws_ref[...], preferred_element_type=F32)
    zst_ref[...] = lax.dot_general(wst_ref[...], hb, NT, preferred_element_type=F32)


def _norm_small(x2, gain, w_small, w_small_t, tm):
    n, d = x2.shape
    return pl.pallas_call(
        _norm_small_kernel,
        grid=(n // tm,),
        in_specs=[pl.BlockSpec((tm, d), lambda i: (i, 0)),
                  pl.BlockSpec((1, d), lambda i: (0, 0)),
                  pl.BlockSpec((d, LANES), lambda i: (0, 0)),
                  pl.BlockSpec((2 * HEADS, d), lambda i: (0, 0))],
        out_specs=[pl.BlockSpec((tm, d), lambda i: (i, 0)),
                   pl.BlockSpec((tm, LANES), lambda i: (i, 0)),
                   pl.BlockSpec((2 * HEADS, tm), lambda i: (0, i))],
        out_shape=[jax.ShapeDtypeStruct((n, d), BF16),
                   jax.ShapeDtypeStruct((n, LANES), F32),
                   jax.ShapeDtypeStruct((2 * HEADS, n), F32)],
        compiler_params=_params("parallel"),
        name="norm_small",
    )(x2, gain, w_small, w_small_t)


def _mm_kernel(a_ref, w_ref, o_ref):
    o_ref[...] = jnp.dot(a_ref[...], w_ref[...], preferred_element_type=F32).astype(o_ref.dtype)


def _matmul(a, w, tm, tn, out_dtype):
    m, k = a.shape
    n = w.shape[1]
    return pl.pallas_call(
        _mm_kernel,
        grid=(m // tm, n // tn),
        in_specs=[pl.BlockSpec((tm, k), lambda i, j: (i, 0)),
                  pl.BlockSpec((k, tn), lambda i, j: (0, j))],
        out_specs=pl.BlockSpec((tm, tn), lambda i, j: (i, j)),
        out_shape=jax.ShapeDtypeStruct((m, n), out_dtype),
        compiler_params=_params("parallel", "arbitrary"),
        name="matmul",
    )(a, w)


def _mm_res_kernel(a_ref, w_ref, r_ref, o_ref):
    o_ref[...] = r_ref[...] + jnp.dot(a_ref[...], w_ref[...], preferred_element_type=F32)


def _matmul_residual(a, w, res, tm, tn):
    m, k = a.shape
    n = w.shape[1]
    return pl.pallas_call(
        _mm_res_kernel,
        grid=(m // tm, n // tn),
        in_specs=[pl.BlockSpec((tm, k), lambda i, j: (i, 0)),
                  pl.BlockSpec((k, tn), lambda i, j: (0, j)),
                  pl.BlockSpec((tm, tn), lambda i, j: (i, j))],
        out_specs=pl.BlockSpec((tm, tn), lambda i, j: (i, j)),
        out_shape=jax.ShapeDtypeStruct((m, n), F32),
        compiler_params=_params("parallel", "arbitrary"),
        name="matmul_residual",
    )(a, w, res)


def _ret_kernel(q_ref, k_ref, v_ref, g_ref, cos_ref, sin_ref, inner_ref, qh_ref, kt_ref, cd_ref,
                o_ref, s_ref, *, n_chunks):
    @pl.when(pl.program_id(1) == 0)
    def _():
        s_ref[...] = jnp.zeros_like(s_ref)

    scale = HEAD_DIM ** -0.5
    c = RET_CHUNK
    each = lambda fn, *lists: [fn(*args) for args in zip(*lists)]
    rot = lambda x, cos, sin: x * cos + pltpu.roll(x, HEAD_DIM // 2, 1) * sin
    for ci in range(n_chunks):
        rows = slice(ci * c, (ci + 1) * c)
        cos, sin = cos_ref[rows, :], sin_ref[rows, :]
        for h0 in range(0, HEADS, RET_HEAD_GROUP):
            hs = list(range(h0, h0 + RET_HEAD_GROUP))
            sls = [slice(hh * HEAD_DIM, (hh + 1) * HEAD_DIM) for hh in hs]
            qr = [rot(q_ref[rows, sl], cos, sin) for sl in sls]
            kr = [rot(k_ref[rows, sl], cos, sin) * scale for sl in sls]
            vb = [v_ref[rows, sl].astype(BF16) for sl in sls]
            state = [s_ref[hh] for hh in hs]
            scores = each(lambda a, b, hh: _dot_nt(a, b) * inner_ref[hh], qr, kr, hs)
            o_inner = each(_dot, scores, vb)
            o_cross = each(lambda a, s, hh: _dot(a * qh_ref[hh], s), qr, state, hs)
            kv = each(lambda a, b, hh: _dot_tn(a * kt_ref[hh], b), kr, vb, hs)
            for hh, s, d in zip(hs, state, kv):
                s_ref[hh] = s * cd_ref[hh][0:1, :] + d
            for sl, a, b in zip(sls, o_inner, o_cross):
                o_ref[rows, sl] = (_rms(a + b) * _silu(g_ref[rows, sl])).astype(BF16)


def _retention_tables(t):
    h = np.arange(HEADS, dtype=np.float64)
    log_gamma = np.log1p(-np.exp2(-5.0 - h))
    pos = np.arange(RET_CHUNK, dtype=np.float64)
    rel = pos[:, None] - pos[None, :]
    inner = np.where(rel >= 0, np.exp(np.maximum(rel, 0.0)[None] * log_gamma[:, None, None]), 0.0)
    k_tail = np.exp((RET_CHUNK - 1.0 - pos)[None, :] * log_gamma[:, None])
    q_head = np.exp((pos + 1.0)[None, :] * log_gamma[:, None])
    chunk_decay = np.exp(RET_CHUNK * log_gamma)
    half = HEAD_DIM // 2
    inv_freq = ROPE_BASE ** (-np.arange(half, dtype=np.float64) / half)
    ang = np.arange(t, dtype=np.float64)[:, None] * inv_freq[None, :]
    cos = np.concatenate([np.cos(ang), np.cos(ang)], axis=1)
    sin = np.concatenate([-np.sin(ang), np.sin(ang)], axis=1)
    bc = lambda a: np.broadcast_to(a[:, :, None], (HEADS, RET_CHUNK, HEAD_DIM))
    f = lambda a: jnp.asarray(np.ascontiguousarray(a), F32)
    return (f(cos), f(sin), f(inner), f(bc(q_head)), f(bc(k_tail)),
            f(np.broadcast_to(chunk_decay[:, None, None], (HEADS, SUBLANES, HEAD_DIM))))


def _retention(z, b, t, tb):
    n = b * t
    nt = t // tb
    cos, sin, inner, q_head, k_tail, chunk_decay = _retention_tables(t)
    col = lambda off: pl.BlockSpec((tb, WIDTH), lambda bi, i, off=off: (bi * nt + i, off // WIDTH))
    pos = pl.BlockSpec((tb, HEAD_DIM), lambda bi, i: (i, 0))
    head_tab = lambda r: pl.BlockSpec((HEADS, r, HEAD_DIM), lambda bi, i: (0, 0, 0))
    return pl.pallas_call(
        functools.partial(_ret_kernel, n_chunks=tb // RET_CHUNK),
        grid=(b, nt),
        in_specs=[col(COL_RQ), col(COL_RK), col(COL_RV), col(COL_RG), pos, pos,
                  head_tab(RET_CHUNK), head_tab(RET_CHUNK), head_tab(RET_CHUNK), head_tab(SUBLANES)],
        out_specs=pl.BlockSpec((tb, WIDTH), lambda bi, i: (bi * nt + i, 0)),
        out_shape=jax.ShapeDtypeStruct((n, WIDTH), BF16),
        scratch_shapes=[pltpu.VMEM((HEADS, HEAD_DIM, HEAD_DIM), F32)],
        compiler_params=_params("parallel", "arbitrary"),
        name="retention",
    )(z, z, z, z, cos, sin, inner, q_head, k_tail, chunk_decay)


def _split2(a):
    hi = a.astype(BF16)
    return hi, (a - hi.astype(F32)).astype(BF16)


def _sb_kernel(q_ref, k_ref, v_ref, qn_ref, kn_ref, u_ref, o_ref, kb_ref, vb_ref, acc_ref, carry_ref, *, prep_rows):
    i = pl.program_id(2)
    t = k_ref.shape[0]
    bq = q_ref.shape[0]

    @pl.when(i == 0)
    def _():
        def prep(c, carry):
            rows = pl.ds(pl.multiple_of(c * prep_rows, prep_rows), prep_rows)
            kb_ref[rows, :] = (_rms(k_ref[rows, :]) * kn_ref[...]).astype(BF16)
            vb_ref[rows, :] = v_ref[rows, :].astype(BF16)
            return carry
        lax.fori_loop(0, t // prep_rows, prep, 0)

    qb = (_rms(q_ref[...]) * qn_ref[...] * HEAD_DIM ** -0.5).astype(BF16)
    u = u_ref[...]
    each = lambda fn, *lists: [fn(*args) for args in zip(*lists)]
    d32 = lambda a, b: jnp.dot(a, b, preferred_element_type=F32)

    def tile_pair(j, diagonal):
        has_far = j >= 1
        starts = [pl.multiple_of(j * bq, bq), pl.multiple_of(jnp.maximum(j - 1, 0) * bq, bq)]
        kt = [kb_ref[pl.ds(s, bq), :] for s in starts]
        vt = [vb_ref[pl.ds(s, bq), :] for s in starts]
        z = [lax.dot_general(qb, x, NT, preferred_element_type=F32) for x in kt]
        log_not = each(lambda x: -(jnp.maximum(x, 0.0) + jnp.log(1.0 + jnp.exp(-jnp.abs(x)))), z)
        if diagonal:
            causal = lax.broadcasted_iota(jnp.int32, (bq, bq), 1) < lax.broadcasted_iota(jnp.int32, (bq, bq), 0)
            log_not[0] = jnp.where(causal, log_not[0], 0.0)
        log_not[1] = jnp.where(has_far, log_not[1], 0.0)
        split = each(_split2, log_not)
        later = [d32(hi, u) + d32(lo, u) for hi, lo in split]
        total = each(lambda x: jnp.sum(x, axis=1, keepdims=True), log_not)
        logw = each(lambda a, b, c: a + b + c, log_not, z, later)
        if diagonal:
            w_near = jnp.where(causal, jnp.exp(logw[0]), 0.0)
            before_far = total[0]
        else:
            old = carry_ref[...]
            w_near = jnp.exp(logw[0] + old)
            before_far = old + total[0]
        w_far = jnp.where(has_far, jnp.exp(logw[1] + before_far), 0.0)
        pv = d32(w_near.astype(BF16), vt[0]) + d32(w_far.astype(BF16), vt[1])
        acc_ref[...] = pv if diagonal else acc_ref[...] + pv
        carry = before_far + total[1]
        carry_ref[...] = carry
        return jnp.max(carry)

    def live(state):
        return (state[0] >= 0) & (state[1] > SB_DEAD_CARRY)

    def step(state):
        return state[0] - 2, tile_pair(state[0], False)

    lax.while_loop(live, step, (i - 2, tile_pair(i, True)))
    o_ref[...] = acc_ref[...].astype(BF16)


def _stick_breaking(z, qn, kn, b, t):
    n = b * t
    bq = _tile(t, SB_TILE)
    nt = t // bq
    j = np.arange(bq)
    u = jnp.asarray(j[:, None] > j[None, :], BF16)
    seq = lambda off: pl.BlockSpec((t, HEAD_DIM), lambda bi, h, i, off=off: (bi, off // HEAD_DIM + h))
    return pl.pallas_call(
        functools.partial(_sb_kernel, prep_rows=bq),
        grid=(b, HEADS, nt),
        in_specs=[pl.BlockSpec((bq, HEAD_DIM), lambda bi, h, i: (bi * nt + i, COL_SQ // HEAD_DIM + h)),
                  seq(COL_SK), seq(COL_SV),
                  pl.BlockSpec((1, HEAD_DIM), lambda bi, h, i: (0, 0)),
                  pl.BlockSpec((1, HEAD_DIM), lambda bi, h, i: (0, 0)),
                  pl.BlockSpec((bq, bq), lambda bi, h, i: (0, 0))],
        out_specs=pl.BlockSpec((bq, HEAD_DIM), lambda bi, h, i: (bi * nt + i, h)),
        out_shape=jax.ShapeDtypeStruct((n, WIDTH), BF16),
        scratch_shapes=[pltpu.VMEM((t, HEAD_DIM), BF16), pltpu.VMEM((t, HEAD_DIM), BF16),
                        pltpu.VMEM((bq, HEAD_DIM), F32), pltpu.VMEM((bq, 1), F32)],
        compiler_params=_params("parallel", "parallel", "arbitrary"),
        name="stick_breaking",
    )(z, z, z, qn, kn, u)


def _gdn_kernel(xq_ref, xk_ref, xv_ref, gz_ref, cq_ref, ck_ref, cv_ref, zs_ref, gat_ref, alog_ref, dt_ref,
                alog_t_ref, dt_t_ref, cum_ref, cum_t_ref, gn_ref, o_ref, bq_ref, bk_ref, bv_ref, s_ref):
    tb = GDN_BLOCK
    c = GDN_CHUNK
    halo = SUBLANES

    @pl.when(pl.program_id(1) == 0)
    def _():
        s_ref[...] = jnp.zeros_like(s_ref)
        for buf in (bq_ref, bk_ref, bv_ref):
            buf[0:halo, :] = jnp.zeros((halo, WIDTH), F32)

    def conv(x_ref, cw_ref, buf):
        buf[halo:halo + tb, :] = x_ref[...]
        y = cw_ref[CONV_WIDTH - 1:CONV_WIDTH, :] * buf[halo:halo + tb, :]
        for tap in range(CONV_WIDTH - 1):
            back = CONV_WIDTH - 1 - tap
            y = y + cw_ref[tap:tap + 1, :] * buf[halo - back:halo - back + tb, :]
        buf[0:halo, :] = buf[tb:tb + halo, :]
        buf[halo:halo + tb, :] = _silu(y)

    conv(xq_ref, cq_ref, bq_ref)
    conv(xk_ref, ck_ref, bk_ref)
    conv(xv_ref, cv_ref, bv_ref)

    zs = zs_ref[...]
    beta_lanes = jax.nn.sigmoid(zs)
    g_lanes = -jnp.exp(alog_ref[...]) * jax.nn.softplus(zs + dt_ref[...])
    gc_lanes = _dot_exact_lhs(cum_ref[...], g_lanes)
    g_t = -jnp.exp(alog_t_ref[...]) * jax.nn.softplus(gat_ref[...] + dt_t_ref[...])
    gc_t = _dot_exact_rhs(g_t, cum_t_ref[...])

    ri = lax.broadcasted_iota(jnp.int32, (tb, tb), 0)
    ci = lax.broadcasted_iota(jnp.int32, (tb, tb), 1)
    same = (ri // c) == (ci // c)
    lower = same & (ri >= ci)
    strict = same & (ri > ci)
    eye = (ri == ci).astype(F32)
    scale = HEAD_DIM ** -0.5
    gn = gn_ref[...]

    def each(fn, *lists):
        return [fn(*args) for args in zip(*lists)]

    for h0 in range(0, HEADS, GDN_HEAD_GROUP):
        hs = list(range(h0, h0 + GDN_HEAD_GROUP))
        sls = [slice(hh * HEAD_DIM, (hh + 1) * HEAD_DIM) for hh in hs]
        q = [bq_ref[halo:halo + tb, sl] for sl in sls]
        k = [bk_ref[halo:halo + tb, sl] for sl in sls]
        v = [bv_ref[halo:halo + tb, sl] for sl in sls]
        q = each(lambda x: x * lax.rsqrt(jnp.sum(x * x, axis=-1, keepdims=True) + EPS) * scale, q)
        k = each(lambda x: x * lax.rsqrt(jnp.sum(x * x, axis=-1, keepdims=True) + EPS), k)
        beta = [beta_lanes[:, hh:hh + 1] for hh in hs]
        gc = [gc_lanes[:, HEADS + hh:HEADS + hh + 1] for hh in hs]
        gc_rows = [gc_t[hh:hh + 1, :] for hh in hs]
        decay = each(lambda gi, gj: jnp.where(lower, jnp.exp(jnp.where(lower, gi - gj, 0.0)), 0.0), gc, gc_rows)
        k_beta = each(lambda x, y: x * y, k, beta)
        kk = each(_dot_nt, k_beta, k)
        a = each(lambda x, d: jnp.where(strict, x * d, 0.0), kk, decay)
        x = each(lambda y: eye - y, a)
        p = each(_dot, a, a)
        x = each(lambda y, d: y + d, x, each(_dot, x, p))
        for _ in range(4):
            p = each(_dot, p, p)
            x = each(lambda y, d: y + d, x, each(_dot, x, p))
        e_gc = each(jnp.exp, gc)
        rhs = each(lambda vv, bb, kb, eg: jnp.concatenate([vv * bb, kb * eg], axis=1), v, beta, k_beta, e_gc)
        uw = each(_dot, x, rhs)
        qk = each(lambda y, d: jnp.where(lower, y * d, 0.0), each(_dot_nt, q, k), decay)
        q_dec = each(lambda y, eg: y * eg, q, e_gc)
        for cc in range(tb // c):
            rows = slice(cc * c, (cc + 1) * c)
            g_last = [g[cc * c + c - 1:cc * c + c, :] for g in gc]
            k_dec = each(lambda kx, gl, g: kx[rows] * jnp.exp(gl - g[rows]), k, g_last, gc)
            state = [s_ref[hh] for hh in hs]
            ws = each(lambda y, s: _dot(y[rows, HEAD_DIM:], s), uw, state)
            v_new = each(lambda y, d: y[rows, :HEAD_DIM] - d, uw, ws)
            o_cross = each(lambda y, s: _dot(y[rows], s), q_dec, state)
            o_intra = each(lambda y, vn: _dot(y[rows, cc * c:(cc + 1) * c], vn), qk, v_new)
            kv = each(_dot_tn, k_dec, v_new)
            for hh, s, gl, d in zip(hs, state, g_last, kv):
                s_ref[hh] = s * jnp.exp(gl) + d
            for sl, oc, oi in zip(sls, o_cross, o_intra):
                o = oc + oi
                o_ref[rows, sl] = (_rms(o) * gn * _silu(gz_ref[rows, sl])).astype(BF16)


def _gated_deltanet(z, zs, zst, conv_w, a_log, dt_bias, out_norm, b, t):
    n = b * t
    tb, c = GDN_BLOCK, GDN_CHUNK
    nt = t // tb
    idx = np.arange(tb)
    same = (idx[:, None] // c) == (idx[None, :] // c)
    cum = (same & (idx[:, None] >= idx[None, :])).astype(np.float32)
    cum, cum_t = jnp.asarray(cum, BF16), jnp.asarray(cum.T, BF16)
    lanes = lambda a1: jnp.pad(a1, (HEADS, LANES - 2 * HEADS))[None, :]
    rows = lambda a1: jnp.broadcast_to(a1[:, None], (HEADS, tb))

    def col(off):
        return pl.BlockSpec((tb, WIDTH), lambda bi, i, off=off: (bi * nt + i, off // WIDTH))

    def ccol(blk):
        return pl.BlockSpec((CONV_WIDTH, WIDTH), lambda bi, i, blk=blk: (0, blk))

    const = lambda shape: pl.BlockSpec(shape, lambda bi, i: (0,) * len(shape))
    return pl.pallas_call(
        _gdn_kernel,
        grid=(b, nt),
        in_specs=[col(COL_GQ), col(COL_GK), col(COL_GV), col(COL_GZ), ccol(0), ccol(1), ccol(2),
                  pl.BlockSpec((tb, LANES), lambda bi, i: (bi * nt + i, 0)),
                  pl.BlockSpec((HEADS, tb), lambda bi, i: (1, bi * nt + i)),
                  const((1, LANES)), const((1, LANES)), const((HEADS, tb)), const((HEADS, tb)),
                  const((tb, tb)), const((tb, tb)), const((1, HEAD_DIM))],
        out_specs=pl.BlockSpec((tb, WIDTH), lambda bi, i: (bi * nt + i, 0)),
        out_shape=jax.ShapeDtypeStruct((n, WIDTH), BF16),
        scratch_shapes=[pltpu.VMEM((tb + SUBLANES, WIDTH), F32)] * 3 + [pltpu.VMEM((HEADS, HEAD_DIM, HEAD_DIM), F32)],
        compiler_params=_params("parallel", "arbitrary"),
        name="gated_deltanet",
    )(z, z, z, z, conv_w, conv_w, conv_w, zs, zst, lanes(a_log), lanes(dt_bias), rows(a_log), rows(dt_bias),
      cum, cum_t, out_norm)


def _merge_kernel(or_ref, os_ref, og_ref, wb_ref, gr_ref, gs_ref, gg_ref, o_ref):
    acc = jax.nn.sigmoid(gr_ref[...]) * jnp.dot(or_ref[...], wb_ref[0], preferred_element_type=F32)
    acc = acc + jax.nn.sigmoid(gs_ref[...]) * jnp.dot(os_ref[...], wb_ref[1], preferred_element_type=F32)
    acc = acc + jax.nn.sigmoid(gg_ref[...]) * jnp.dot(og_ref[...], wb_ref[2], preferred_element_type=F32)
    o_ref[...] = acc.astype(BF16)


def _merge(o_r, o_s, o_g, w_branch, z, d, tm, tn):
    n = o_r.shape[0]
    assert COL_GATES % tn == 0 and d % tn == 0
    branch = pl.BlockSpec((tm, WIDTH), lambda j, i: (i, 0))
    gate = lambda bidx: pl.BlockSpec((tm, tn), lambda j, i, bidx=bidx: (i, (COL_GATES + bidx * d) // tn + j))
    return pl.pallas_call(
        _merge_kernel,
        grid=(d // tn, n // tm),
        in_specs=[branch, branch, branch,
                  pl.BlockSpec((3, WIDTH, tn), lambda j, i: (0, 0, j)),
                  gate(0), gate(1), gate(2)],
        out_specs=pl.BlockSpec((tm, tn), lambda j, i: (i, j)),
        out_shape=jax.ShapeDtypeStruct((n, d), BF16),
        compiler_params=_params("parallel", "parallel"),
        name="branch_merge",
    )(o_r, o_s, o_g, w_branch, z, z, z)


def _norm_kernel(x_ref, g_ref, h_ref):
    h_ref[...] = (_rms(x_ref[...]) * g_ref[...]).astype(h_ref.dtype)


def _norm(x2, gain, tm):
    n, d = x2.shape
    return pl.pallas_call(
        _norm_kernel,
        grid=(n // tm,),
        in_specs=[pl.BlockSpec((tm, d), lambda i: (i, 0)), pl.BlockSpec((1, d), lambda i: (0, 0))],
        out_specs=pl.BlockSpec((tm, d), lambda i: (i, 0)),
        out_shape=jax.ShapeDtypeStruct((n, d), BF16),
        compiler_params=_params("parallel"),
        name="norm",
    )(x2, gain)


def _norm_router_kernel(x_ref, g_ref, wr_ref, h_ref, rt_ref):
    h = _rms(x_ref[...]) * g_ref[...]
    h_ref[...] = h
    logits = jnp.dot(h, wr_ref[...], preferred_element_type=F32, precision=lax.Precision.HIGHEST)
    lane = lax.broadcasted_iota(jnp.int32, logits.shape, 1)
    neg = jnp.float32(-jnp.inf)
    lg = jnp.where(lane < N_EXPERTS, logits, neg)
    m1 = jnp.max(lg, axis=-1, keepdims=True)
    i1 = jnp.min(jnp.where(lg == m1, lane, LANES), axis=-1, keepdims=True)
    lg2 = jnp.where(lane == i1, neg, lg)
    m2 = jnp.max(lg2, axis=-1, keepdims=True)
    i2 = jnp.min(jnp.where(lg2 == m2, lane, LANES), axis=-1, keepdims=True)
    e = jnp.exp(m2 - m1)
    g1 = 1.0 / (1.0 + e)
    g2 = e / (1.0 + e)
    rt_ref[...] = jnp.where(lane == 0, i1.astype(F32),
                            jnp.where(lane == 1, i2.astype(F32),
                                      jnp.where(lane == 2, g1, jnp.where(lane == 3, g2, 0.0))))


def _norm_router(x2, gain, w_router_pad, tm):
    n, d = x2.shape
    return pl.pallas_call(
        _norm_router_kernel,
        grid=(n // tm,),
        in_specs=[pl.BlockSpec((tm, d), lambda i: (i, 0)), pl.BlockSpec((1, d), lambda i: (0, 0)),
                  pl.BlockSpec((d, LANES), lambda i: (0, 0))],
        out_specs=[pl.BlockSpec((tm, d), lambda i: (i, 0)), pl.BlockSpec((tm, LANES), lambda i: (i, 0))],
        out_shape=[jax.ShapeDtypeStruct((n, d), F32), jax.ShapeDtypeStruct((n, LANES), F32)],
        compiler_params=_params("parallel"),
        name="norm_router",
    )(x2, gain, w_router_pad)


def _swiglu_kernel(be_ref, nu_ref, x_ref, wg_ref, wu_ref, wd_ref, *rest, residual):
    if residual:
        r_ref, o_ref, xb_ref = rest
    else:
        o_ref, xb_ref = rest
    blk, f = pl.program_id(0), pl.program_id(1)

    @pl.when(blk < nu_ref[0])
    def _():
        @pl.when(f == 0)
        def _():
            xb_ref[...] = x_ref[...].astype(BF16)
            o_ref[...] = r_ref[...] if residual else jnp.zeros_like(o_ref)

        xb = xb_ref[...]
        gate = jnp.dot(xb, wg_ref[0], preferred_element_type=F32)
        up = jnp.dot(xb, wu_ref[0], preferred_element_type=F32)
        act = (_silu(gate) * up).astype(BF16)
        o_ref[...] += jnp.dot(act, wd_ref[0], preferred_element_type=F32)

    @pl.when((blk >= nu_ref[0]) & (f == 0))
    def _():
        o_ref[...] = jnp.zeros_like(o_ref)


def _grouped_swiglu(x, w_gate, w_up, w_down, block_expert, n_used, bm, tf, residual=None):
    n, d = x.shape
    d_ff = w_gate.shape[2]
    nb, nf = n // bm, d_ff // tf

    def row(blk, f, be, nu):
        return (jnp.minimum(blk, nu[0] - 1), 0)

    def f_idx(blk, f, nu):
        return jnp.where(blk < nu[0], f, nf - 1)

    def w_in_map(blk, f, be, nu):
        return (be[jnp.minimum(blk, nu[0] - 1)], 0, f_idx(blk, f, nu))

    def w_out_map(blk, f, be, nu):
        return (be[jnp.minimum(blk, nu[0] - 1)], f_idx(blk, f, nu), 0)

    in_specs = [pl.BlockSpec((bm, d), row),
                pl.BlockSpec((1, d, tf), w_in_map), pl.BlockSpec((1, d, tf), w_in_map),
                pl.BlockSpec((1, tf, d), w_out_map)]
    args = [x, w_gate, w_up, w_down]
    if residual is not None:
        in_specs.append(pl.BlockSpec((bm, d), row))
        args.append(residual)
    return pl.pallas_call(
        functools.partial(_swiglu_kernel, residual=residual is not None),
        grid_spec=pltpu.PrefetchScalarGridSpec(
            num_scalar_prefetch=2, grid=(nb, nf), in_specs=in_specs,
            out_specs=pl.BlockSpec((bm, d), lambda blk, f, be, nu: (blk, 0)),
            scratch_shapes=[pltpu.VMEM((bm, d), BF16)]),
        out_shape=jax.ShapeDtypeStruct((n, d), F32),
        compiler_params=_params("arbitrary", "arbitrary"),
        name="grouped_swiglu",
    )(block_expert, n_used, *args)


def _row_copy(src_hbm, dst_ref, src_row, dst_row, sem):
    return pltpu.make_async_copy(src_hbm.at[pl.ds(src_row, 1), :], dst_ref.at[pl.ds(dst_row, 1), :], sem)


def _gather_kernel(idx_ref, src_hbm, o_ref, sem, *, bm):
    base = pl.program_id(0) * bm

    def start(r, carry):
        _row_copy(src_hbm, o_ref, idx_ref[base + r], r, sem).start()
        return carry

    def wait(r, carry):
        _row_copy(src_hbm, o_ref, 0, r, sem).wait()
        return carry

    lax.fori_loop(0, bm, start, 0)
    lax.fori_loop(0, bm, wait, 0)


def _gather_rows(src, idx, bm):
    d = src.shape[1]
    n_out = idx.shape[0]
    return pl.pallas_call(
        functools.partial(_gather_kernel, bm=bm),
        grid_spec=pltpu.PrefetchScalarGridSpec(
            num_scalar_prefetch=1, grid=(n_out // bm,),
            in_specs=[pl.BlockSpec(memory_space=pl.ANY)],
            out_specs=pl.BlockSpec((bm, d), lambda i, idx_ref: (i, 0)),
            scratch_shapes=[pltpu.SemaphoreType.DMA(())]),
        out_shape=jax.ShapeDtypeStruct((n_out, d), src.dtype),
        compiler_params=_params("arbitrary"),
        name="gather_rows",
    )(idx, src)


def _combine_kernel(pos_ref, x_ref, rt_ref, y_hbm, o_ref, b0_ref, b1_ref, sem, *, tm):
    base = pl.program_id(0) * tm

    def start(r, carry):
        _row_copy(y_hbm, b0_ref, pos_ref[2 * (base + r)], r, sem).start()
        _row_copy(y_hbm, b1_ref, pos_ref[2 * (base + r) + 1], r, sem).start()
        return carry

    def wait(r, carry):
        _row_copy(y_hbm, b0_ref, 0, r, sem).wait()
        _row_copy(y_hbm, b1_ref, 0, r, sem).wait()
        return carry

    lax.fori_loop(0, tm, start, 0)
    lax.fori_loop(0, tm, wait, 0)
    rt = rt_ref[...]
    o_ref[...] = x_ref[...] + (b0_ref[...] * rt[:, 2:3] + b1_ref[...] * rt[:, 3:4])


def _combine(x2, route, y_sorted, pos, tm):
    n, d = x2.shape
    return pl.pallas_call(
        functools.partial(_combine_kernel, tm=tm),
        grid_spec=pltpu.PrefetchScalarGridSpec(
            num_scalar_prefetch=1, grid=(n // tm,),
            in_specs=[pl.BlockSpec((tm, d), lambda i, p: (i, 0)),
                      pl.BlockSpec((tm, LANES), lambda i, p: (i, 0)),
                      pl.BlockSpec(memory_space=pl.ANY)],
            out_specs=pl.BlockSpec((tm, d), lambda i, p: (i, 0)),
            scratch_shapes=[pltpu.VMEM((tm, d), F32), pltpu.VMEM((tm, d), F32), pltpu.SemaphoreType.DMA(())]),
        out_shape=jax.ShapeDtypeStruct((n, d), F32),
        compiler_params=_params("arbitrary"),
        name="combine",
    )(pos, x2, route, y_sorted)


def _tile(n, want):
    for cand in range(min(n, want), 0, -LANES):
        if n % cand == 0:
            return cand
    raise ValueError((n, want))


def _mixer(x2, b, t, norm_w, w_in, sb_qn, sb_kn, conv_w, a_log, dt_bias, gdn_on, w_branch, w_out):
    n, d = x2.shape
    w_main = jnp.concatenate([w_in[:, :SMALL_OFF], w_in[:, SMALL_OFF + 2 * HEADS:]], axis=1).astype(BF16)
    w_small = w_in[:, SMALL_OFF:SMALL_OFF + 2 * HEADS]
    w_small_pad = jnp.pad(w_small, ((0, 0), (0, LANES - 2 * HEADS))).astype(BF16)
    hidden, zs, zst = _norm_small(x2, norm_w[None, :], w_small_pad, w_small.T.astype(BF16), _tile(n, 512))
    z = _matmul(hidden, w_main, _tile(n, 1024), _tile(w_main.shape[1], 1024), F32)
    o_r = _retention(z, b, t, _tile(t, 512))
    o_s = _stick_breaking(z, sb_qn[None, :], sb_kn[None, :], b, t)
    o_g = _gated_deltanet(z, zs, zst, conv_w, a_log, dt_bias, gdn_on[None, :], b, t)
    merged = _merge(o_r, o_s, o_g, w_branch.astype(BF16), z, d, _tile(n, 512), _tile(d, 1024))
    return _matmul_residual(merged, w_out.astype(BF16), x2, _tile(n, 1024), _tile(d, 1024))


def _dense_ffn(x2, norm_w, w_gate, w_up, w_down, bm, tf):
    n = x2.shape[0]
    h = _norm(x2, norm_w[None, :], _tile(n, 512))
    nb = n // bm
    return _grouped_swiglu(h, w_gate[None].astype(BF16), w_up[None].astype(BF16), w_down[None].astype(BF16),
                           jnp.zeros((nb,), jnp.int32), jnp.full((1,), nb, jnp.int32), bm, tf, residual=x2)


def _moe_ffn(x2, norm_w, w_router, w_gate, w_up, w_down, bm, tf):
    n, d = x2.shape
    h, route = _norm_router(x2, norm_w[None, :], jnp.pad(w_router, ((0, 0), (0, LANES - N_EXPERTS))),
                            _tile(n, 512))
    flat_e = route[:, :TOP_K].astype(jnp.int32).reshape(-1)
    n_assign = n * TOP_K
    onehot = (flat_e[:, None] == jnp.arange(N_EXPERTS, dtype=jnp.int32)[None, :]).astype(jnp.int32)
    running = jnp.cumsum(onehot, axis=0)
    counts = running[-1]
    rank = jnp.sum((running - onehot) * onehot, axis=1)
    padded = (counts + bm - 1) // bm * bm
    pad_end = jnp.cumsum(padded)
    pad_start = pad_end - padded
    dest = (pad_start[flat_e] + rank).astype(jnp.int32)
    n_blocks = -(-n_assign // bm) + N_EXPERTS
    n_slots = n_blocks * bm
    flat_tok = jnp.arange(n_assign, dtype=jnp.int32) // TOP_K
    slot_tok = jnp.zeros((n_slots,), jnp.int32).at[dest].set(flat_tok)
    block_e = jnp.minimum(jnp.searchsorted(pad_end, jnp.arange(n_blocks, dtype=jnp.int32) * bm, side='right'),
                          N_EXPERTS - 1).astype(jnp.int32)
    n_used = (pad_end[-1:] // bm).astype(jnp.int32)
    x_sorted = _gather_rows(h, slot_tok, bm)
    y_sorted = _grouped_swiglu(x_sorted, w_gate.astype(BF16), w_up.astype(BF16), w_down.astype(BF16),
                               block_e, n_used, bm, tf)
    return _combine(x2, route, y_sorted, dest, _tile(n, 256))


def kernel(x, mix_norm, w_in, sb_q_norm, sb_k_norm, gdn_conv, gdn_a_log, gdn_dt_bias, gdn_out_norm, w_branch,
           w_out, ffn_norm, w_ffn_gate, w_ffn_up, w_ffn_down, w_router, w_exp_gate, w_exp_up, w_exp_down):
    b, t, d = x.shape
    depth = mix_norm.shape[0]
    d_ff = w_ffn_gate.shape[2]
    x2 = x.reshape(b * t, d)
    bm = _tile(b * t, 512)
    tf = _tile(d_ff, 512)
    for layer in range(depth):
        x2 = _mixer(x2, b, t, mix_norm[layer], w_in[layer], sb_q_norm[layer], sb_k_norm[layer], gdn_conv[layer],
                    gdn_a_log[layer], gdn_dt_bias[layer], gdn_out_norm[layer], w_branch[layer], w_out[layer])
        i = layer // 2
        if layer % 2 == 0:
            x2 = _dense_ffn(x2, ffn_norm[layer], w_ffn_gate[i], w_ffn_up[i], w_ffn_down[i], bm, tf)
        else:
            x2 = _moe_ffn(x2, ffn_norm[layer], w_router[i], w_exp_gate[i], w_exp_up[i], w_exp_down[i], bm, tf)
    return x2.reshape(b, t, d)
```

```python
import functools

import numpy as np
import jax
import jax.numpy as jnp
from jax import lax
from jax.experimental import pallas as pl
from jax.experimental.pallas import tpu as pltpu

F32 = jnp.float32
BF16 = jnp.bfloat16

HEAD_DIM = 128
HEADS = 8
WIDTH = HEADS * HEAD_DIM
RET_CHUNK = 128
RET_HEAD_GROUP = 4
SB_TILE = 256
SB_HEADS = 2
SB_DEAD_CARRY = -104.0
GDN_CHUNK = 64
GDN_BLOCK = 256
GDN_HEAD_GROUP = 4
CONV_WIDTH = 4
ROPE_BASE = 10000.0
N_EXPERTS = 8
MOE_GATHER_STEPS = 8
TOP_K = 2
EPS = 1e-6
LANES = 128
SUBLANES = 8
VMEM_LIMIT = 56 * 1024 * 1024

COL_RQ, COL_RK, COL_RV, COL_RG = 0, WIDTH, 2 * WIDTH, 3 * WIDTH
COL_SQ, COL_SK, COL_SV = 4 * WIDTH, 5 * WIDTH, 6 * WIDTH
COL_GQ, COL_GK, COL_GV, COL_GZ = 7 * WIDTH, 8 * WIDTH, 9 * WIDTH, 10 * WIDTH
COL_GATES = 11 * WIDTH
SMALL_OFF = 10 * WIDTH

NT = (((1,), (1,)), ((), ()))
TN = (((0,), (0,)), ((), ()))


def _params(*sem):
    return pltpu.CompilerParams(dimension_semantics=sem, vmem_limit_bytes=VMEM_LIMIT)


def _dot(a, b):
    return jnp.dot(a.astype(BF16), b.astype(BF16), preferred_element_type=F32)


def _dot_nt(a, b):
    return lax.dot_general(a.astype(BF16), b.astype(BF16), NT, preferred_element_type=F32)


def _dot_tn(a, b):
    return lax.dot_general(a.astype(BF16), b.astype(BF16), TN, preferred_element_type=F32)


def _split3(a):
    hi = a.astype(BF16)
    r1 = a - hi.astype(F32)
    mid = r1.astype(BF16)
    lo = (r1 - mid.astype(F32)).astype(BF16)
    return hi, mid, lo


def _dot_exact_rhs(a, b01):
    return sum(jnp.dot(t, b01, preferred_element_type=F32) for t in _split3(a))


def _dot_exact_lhs(a01, b):
    return sum(jnp.dot(a01, t, preferred_element_type=F32) for t in _split3(b))


def _silu(x):
    return x * jax.nn.sigmoid(x)


def _rms(x):
    return x * lax.rsqrt(jnp.mean(x * x, axis=-1, keepdims=True) + EPS)


def _norm_small_kernel(x_ref, g_ref, ws_ref, wst_ref, h_ref, zs_ref, zst_ref):
    hb = (_rms(x_ref[...]) * g_ref[...]).astype(BF16)
    h_ref[...] = hb
    zs_ref[...] = jnp.dot(hb, ws_ref[...], preferred_element_type=F32)
    zst_ref[...] = lax.dot_general(wst_ref[...], hb, NT, preferred_element_type=F32)


def _norm_small(x2, gain, w_small, w_small_t, tm):
    n, d = x2.shape
    return pl.pallas_call(
        _norm_small_kernel,
        grid=(n // tm,),
        in_specs=[pl.BlockSpec((tm, d), lambda i: (i, 0)),
                  pl.BlockSpec((1, d), lambda i: (0, 0)),
                  pl.BlockSpec((d, LANES), lambda i: (0, 0)),
                  pl.BlockSpec((2 * HEADS, d), lambda i: (0, 0))],
        out_specs=[pl.BlockSpec((tm, d), lambda i: (i, 0)),
                   pl.BlockSpec((tm, LANES), lambda i: (i, 0)),
                   pl.BlockSpec((2 * HEADS, tm), lambda i: (0, i))],
        out_shape=[jax.ShapeDtypeStruct((n, d), BF16),
                   jax.ShapeDtypeStruct((n, LANES), F32),
                   jax.ShapeDtypeStruct((2 * HEADS, n), F32)],
        compiler_params=_params("parallel"),
        name="norm_small",
    )(x2, gain, w_small, w_small_t)


def _mm_kernel(a_ref, w_ref, o_ref):
    o_ref[...] = jnp.dot(a_ref[...], w_ref[...], preferred_element_type=F32).astype(o_ref.dtype)


def _matmul(a, w, tm, tn, out_dtype):
    m, k = a.shape
    n = w.shape[1]
    return pl.pallas_call(
        _mm_kernel,
        grid=(m // tm, n // tn),
        in_specs=[pl.BlockSpec((tm, k), lambda i, j: (i, 0)),
                  pl.BlockSpec((k, tn), lambda i, j: (0, j))],
        out_specs=pl.BlockSpec((tm, tn), lambda i, j: (i, j)),
        out_shape=jax.ShapeDtypeStruct((m, n), out_dtype),
        compiler_params=_params("parallel", "arbitrary"),
        name="matmul",
    )(a, w)


def _mm_res_kernel(a_ref, w_ref, r_ref, o_ref):
    o_ref[...] = r_ref[...] + jnp.dot(a_ref[...], w_ref[...], preferred_element_type=F32)


def _matmul_residual(a, w, res, tm, tn):
    m, k = a.shape
    n = w.shape[1]
    return pl.pallas_call(
        _mm_res_kernel,
        grid=(m // tm, n // tn),
        in_specs=[pl.BlockSpec((tm, k), lambda i, j: (i, 0)),
                  pl.BlockSpec((k, tn), lambda i, j: (0, j)),
                  pl.BlockSpec((tm, tn), lambda i, j: (i, j))],
        out_specs=pl.BlockSpec((tm, tn), lambda i, j: (i, j)),
        out_shape=jax.ShapeDtypeStruct((m, n), F32),
        compiler_params=_params("parallel", "arbitrary"),
        name="matmul_residual",
    )(a, w, res)


def _ret_kernel(q_ref, k_ref, v_ref, g_ref, cos_ref, sin_ref, inner_ref, qh_ref, kt_ref, cd_ref,
                o_ref, s_ref, *, n_chunks):
    @pl.when(pl.program_id(1) == 0)
    def _():
        s_ref[...] = jnp.zeros_like(s_ref)

    scale = HEAD_DIM ** -0.5
    c = RET_CHUNK
    each = lambda fn, *lists: [fn(*args) for args in zip(*lists)]
    rot = lambda x, cos, sin: x * cos + pltpu.roll(x, HEAD_DIM // 2, 1) * sin
    for ci in range(n_chunks):
        rows = slice(ci * c, (ci + 1) * c)
        cos, sin = cos_ref[rows, :], sin_ref[rows, :]
        for h0 in range(0, HEADS, RET_HEAD_GROUP):
            hs = list(range(h0, h0 + RET_HEAD_GROUP))
            sls = [slice(hh * HEAD_DIM, (hh + 1) * HEAD_DIM) for hh in hs]
            qr = [rot(q_ref[rows, sl], cos, sin) for sl in sls]
            kr = [rot(k_ref[rows, sl], cos, sin) * scale for sl in sls]
            vb = [v_ref[rows, sl].astype(BF16) for sl in sls]
            state = [s_ref[hh] for hh in hs]
            scores = each(lambda a, b, hh: _dot_nt(a, b) * inner_ref[hh], qr, kr, hs)
            o_inner = each(_dot, scores, vb)
            o_cross = each(lambda a, s, hh: _dot(a * qh_ref[hh], s), qr, state, hs)
            kv = each(lambda a, b, hh: _dot_tn(a * kt_ref[hh], b), kr, vb, hs)
            for hh, s, d in zip(hs, state, kv):
                s_ref[hh] = s * cd_ref[hh][0:1, :] + d
            for sl, a, b in zip(sls, o_inner, o_cross):
                o_ref[rows, sl] = (_rms(a + b) * _silu(g_ref[rows, sl])).astype(BF16)


def _retention_tables(t):
    h = np.arange(HEADS, dtype=np.float64)
    log_gamma = np.log1p(-np.exp2(-5.0 - h))
    pos = np.arange(RET_CHUNK, dtype=np.float64)
    rel = pos[:, None] - pos[None, :]
    inner = np.where(rel >= 0, np.exp(np.maximum(rel, 0.0)[None] * log_gamma[:, None, None]), 0.0)
    k_tail = np.exp((RET_CHUNK - 1.0 - pos)[None, :] * log_gamma[:, None])
    q_head = np.exp((pos + 1.0)[None, :] * log_gamma[:, None])
    chunk_decay = np.exp(RET_CHUNK * log_gamma)
    half = HEAD_DIM // 2
    inv_freq = ROPE_BASE ** (-np.arange(half, dtype=np.float64) / half)
    ang = np.arange(t, dtype=np.float64)[:, None] * inv_freq[None, :]
    cos = np.concatenate([np.cos(ang), np.cos(ang)], axis=1)
    sin = np.concatenate([-np.sin(ang), np.sin(ang)], axis=1)
    bc = lambda a: np.broadcast_to(a[:, :, None], (HEADS, RET_CHUNK, HEAD_DIM))
    f = lambda a: jnp.asarray(np.ascontiguousarray(a), F32)
    return (f(cos), f(sin), f(inner), f(bc(q_head)), f(bc(k_tail)),
            f(np.broadcast_to(chunk_decay[:, None, None], (HEADS, SUBLANES, HEAD_DIM))))


def _retention(z, b, t, tb):
    n = b * t
    nt = t // tb
    cos, sin, inner, q_head, k_tail, chunk_decay = _retention_tables(t)
    col = lambda off: pl.BlockSpec((tb, WIDTH), lambda bi, i, off=off: (bi * nt + i, off // WIDTH))
    pos = pl.BlockSpec((tb, HEAD_DIM), lambda bi, i: (i, 0))
    head_tab = lambda r: pl.BlockSpec((HEADS, r, HEAD_DIM), lambda bi, i: (0, 0, 0))
    return pl.pallas_call(
        functools.partial(_ret_kernel, n_chunks=tb // RET_CHUNK),
        grid=(b, nt),
        in_specs=[col(COL_RQ), col(COL_RK), col(COL_RV), col(COL_RG), pos, pos,
                  head_tab(RET_CHUNK), head_tab(RET_CHUNK), head_tab(RET_CHUNK), head_tab(SUBLANES)],
        out_specs=pl.BlockSpec((tb, WIDTH), lambda bi, i: (bi * nt + i, 0)),
        out_shape=jax.ShapeDtypeStruct((n, WIDTH), BF16),
        scratch_shapes=[pltpu.VMEM((HEADS, HEAD_DIM, HEAD_DIM), F32)],
        compiler_params=_params("parallel", "arbitrary"),
        name="retention",
    )(z, z, z, z, cos, sin, inner, q_head, k_tail, chunk_decay)


def _split2(a):
    hi = a.astype(BF16)
    return hi, (a - hi.astype(F32)).astype(BF16)


def _sb_kernel(q_ref, k_ref, v_ref, qn_ref, kn_ref, u_ref, o_ref, kb_ref, vb_ref, acc_ref, carry_ref, *, prep_rows):
    i = pl.program_id(2)
    t = k_ref.shape[0]
    bq = q_ref.shape[0]
    heads = [slice(h * HEAD_DIM, (h + 1) * HEAD_DIM) for h in range(SB_HEADS)]

    @pl.when(i == 0)
    def _():
        def prep(c, carry):
            rows = pl.ds(pl.multiple_of(c * prep_rows, prep_rows), prep_rows)
            for sl in heads:
                kb_ref[rows, sl] = (_rms(k_ref[rows, sl]) * kn_ref[...]).astype(BF16)
            vb_ref[rows, :] = v_ref[rows, :].astype(BF16)
            return carry
        lax.fori_loop(0, t // prep_rows, prep, 0)

    qb = [(_rms(q_ref[:, sl]) * qn_ref[...] * HEAD_DIM ** -0.5).astype(BF16) for sl in heads]
    u = u_ref[...]
    each = lambda fn, *lists: [fn(*args) for args in zip(*lists)]
    d32 = lambda a, b: jnp.dot(a, b, preferred_element_type=F32)
    streams = [(h, far) for h in range(SB_HEADS) for far in (0, 1)]

    def tile_pair(j, diagonal):
        has_far = j >= 1
        starts = [pl.multiple_of(j * bq, bq), pl.multiple_of(jnp.maximum(j - 1, 0) * bq, bq)]
        z = [lax.dot_general(qb[h], kb_ref[pl.ds(starts[far], bq), heads[h]], NT, preferred_element_type=F32)
             for h, far in streams]
        log_not = each(lambda x: -(jnp.maximum(x, 0.0) + jnp.log(1.0 + jnp.exp(-jnp.abs(x)))), z)
        if diagonal:
            causal = lax.broadcasted_iota(jnp.int32, (bq, bq), 1) < lax.broadcasted_iota(jnp.int32, (bq, bq), 0)
        for n, (h, far) in enumerate(streams):
            if far:
                log_not[n] = jnp.where(has_far, log_not[n], 0.0)
            elif diagonal:
                log_not[n] = jnp.where(causal, log_not[n], 0.0)
        split = each(_split2, log_not)
        later = [d32(hi, u) + d32(lo, u) for hi, lo in split]
        total = each(lambda x: jnp.sum(x, axis=1, keepdims=True), log_not)
        logw = each(lambda a, b, c: a + b + c, log_not, z, later)
        tops = []
        for h in range(SB_HEADS):
            near, far = 2 * h, 2 * h + 1
            if diagonal:
                w_near = jnp.where(causal, jnp.exp(logw[near]), 0.0)
                before_far = total[near]
            else:
                old = carry_ref[h]
                w_near = jnp.exp(logw[near] + old)
                before_far = old + total[near]
            w_far = jnp.where(has_far, jnp.exp(logw[far] + before_far), 0.0)
            pv = (d32(w_near.astype(BF16), vb_ref[pl.ds(starts[0], bq), heads[h]])
                  + d32(w_far.astype(BF16), vb_ref[pl.ds(starts[1], bq), heads[h]]))
            acc_ref[:, heads[h]] = pv if diagonal else acc_ref[:, heads[h]] + pv
            carry = before_far + total[far]
            carry_ref[h] = carry
            tops.append(jnp.max(carry))
        return functools.reduce(jnp.maximum, tops)

    def live(state):
        return (state[0] >= 0) & (state[1] > SB_DEAD_CARRY)

    def step(state):
        return state[0] - 2, tile_pair(state[0], False)

    lax.while_loop(live, step, (i - 2, tile_pair(i, True)))
    o_ref[...] = acc_ref[...].astype(BF16)


def _stick_breaking(z, qn, kn, b, t):
    n = b * t
    bq = _tile(t, SB_TILE)
    nt = t // bq
    w = SB_HEADS * HEAD_DIM
    j = np.arange(bq)
    u = jnp.asarray(j[:, None] > j[None, :], BF16)
    seq = lambda off: pl.BlockSpec((t, w), lambda bi, g, i, off=off: (bi, off // w + g))
    return pl.pallas_call(
        functools.partial(_sb_kernel, prep_rows=bq),
        grid=(b, HEADS // SB_HEADS, nt),
        in_specs=[pl.BlockSpec((bq, w), lambda bi, g, i: (bi * nt + i, COL_SQ // w + g)),
                  seq(COL_SK), seq(COL_SV),
                  pl.BlockSpec((1, HEAD_DIM), lambda bi, g, i: (0, 0)),
                  pl.BlockSpec((1, HEAD_DIM), lambda bi, g, i: (0, 0)),
                  pl.BlockSpec((bq, bq), lambda bi, g, i: (0, 0))],
        out_specs=pl.BlockSpec((bq, w), lambda bi, g, i: (bi * nt + i, g)),
        out_shape=jax.ShapeDtypeStruct((n, WIDTH), BF16),
        scratch_shapes=[pltpu.VMEM((t, w), BF16), pltpu.VMEM((t, w), BF16),
                        pltpu.VMEM((bq, w), F32), pltpu.VMEM((SB_HEADS, bq, 1), F32)],
        compiler_params=_params("parallel", "parallel", "arbitrary"),
        name="stick_breaking",
    )(z, z, z, qn, kn, u)


def _gdn_kernel(xq_ref, xk_ref, xv_ref, gz_ref, cq_ref, ck_ref, cv_ref, zs_ref, gat_ref, alog_ref, dt_ref,
                alog_t_ref, dt_t_ref, cum_ref, cum_t_ref, gn_ref, o_ref, bq_ref, bk_ref, bv_ref, s_ref):
    tb = GDN_BLOCK
    c = GDN_CHUNK
    halo = SUBLANES

    @pl.when(pl.program_id(1) == 0)
    def _():
        s_ref[...] = jnp.zeros_like(s_ref)
        for buf in (bq_ref, bk_ref, bv_ref):
            buf[0:halo, :] = jnp.zeros((halo, WIDTH), F32)

    def conv(x_ref, cw_ref, buf):
        buf[halo:halo + tb, :] = x_ref[...]
        y = cw_ref[CONV_WIDTH - 1:CONV_WIDTH, :] * buf[halo:halo + tb, :]
        for tap in range(CONV_WIDTH - 1):
            back = CONV_WIDTH - 1 - tap
            y = y + cw_ref[tap:tap + 1, :] * buf[halo - back:halo - back + tb, :]
        buf[0:halo, :] = buf[tb:tb + halo, :]
        buf[halo:halo + tb, :] = _silu(y)

    conv(xq_ref, cq_ref, bq_ref)
    conv(xk_ref, ck_ref, bk_ref)
    conv(xv_ref, cv_ref, bv_ref)

    zs = zs_ref[...]
    beta_lanes = jax.nn.sigmoid(zs)
    g_lanes = -jnp.exp(alog_ref[...]) * jax.nn.softplus(zs + dt_ref[...])
    gc_lanes = _dot_exact_lhs(cum_ref[...], g_lanes)
    g_t = -jnp.exp(alog_t_ref[...]) * jax.nn.softplus(gat_ref[...] + dt_t_ref[...])
    gc_t = _dot_exact_rhs(g_t, cum_t_ref[...])

    ri = lax.broadcasted_iota(jnp.int32, (tb, tb), 0)
    ci = lax.broadcasted_iota(jnp.int32, (tb, tb), 1)
    same = (ri // c) == (ci // c)
    lower = same & (ri >= ci)
    strict = same & (ri > ci)
    eye = (ri == ci).astype(F32)
    scale = HEAD_DIM ** -0.5
    gn = gn_ref[...]

    def each(fn, *lists):
        return [fn(*args) for args in zip(*lists)]

    for h0 in range(0, HEADS, GDN_HEAD_GROUP):
        hs = list(range(h0, h0 + GDN_HEAD_GROUP))
        sls = [slice(hh * HEAD_DIM, (hh + 1) * HEAD_DIM) for hh in hs]
        q = [bq_ref[halo:halo + tb, sl] for sl in sls]
        k = [bk_ref[halo:halo + tb, sl] for sl in sls]
        v = [bv_ref[halo:halo + tb, sl] for sl in sls]
        q = each(lambda x: x * lax.rsqrt(jnp.sum(x * x, axis=-1, keepdims=True) + EPS) * scale, q)
        k = each(lambda x: x * lax.rsqrt(jnp.sum(x * x, axis=-1, keepdims=True) + EPS), k)
        beta = [beta_lanes[:, hh:hh + 1] for hh in hs]
        gc = [gc_lanes[:, HEADS + hh:HEADS + hh + 1] for hh in hs]
        gc_rows = [gc_t[hh:hh + 1, :] for hh in hs]
        decay = each(lambda gi, gj: jnp.where(lower, jnp.exp(jnp.where(lower, gi - gj, 0.0)), 0.0), gc, gc_rows)
        k_beta = each(lambda x, y: x * y, k, beta)
        kk = each(_dot_nt, k_beta, k)
        a = each(lambda x, d: jnp.where(strict, x * d, 0.0), kk, decay)
        x = each(lambda y: eye - y, a)
        p = each(_dot, a, a)
        x = each(lambda y, d: y + d, x, each(_dot, x, p))
        for _ in range(4):
            p = each(_dot, p, p)
            x = each(lambda y, d: y + d, x, each(_dot, x, p))
        e_gc = each(jnp.exp, gc)
        rhs = each(lambda vv, bb, kb, eg: jnp.concatenate([vv * bb, kb * eg], axis=1), v, beta, k_beta, e_gc)
        uw = each(_dot, x, rhs)
        qk = each(lambda y, d: jnp.where(lower, y * d, 0.0), each(_dot_nt, q, k), decay)
        q_dec = each(lambda y, eg: y * eg, q, e_gc)
        for cc in range(tb // c):
            rows = slice(cc * c, (cc + 1) * c)
            g_last = [g[cc * c + c - 1:cc * c + c, :] for g in gc]
            k_dec = each(lambda kx, gl, g: kx[rows] * jnp.exp(gl - g[rows]), k, g_last, gc)
            state = [s_ref[hh] for hh in hs]
            ws = each(lambda y, s: _dot(y[rows, HEAD_DIM:], s), uw, state)
            v_new = each(lambda y, d: y[rows, :HEAD_DIM] - d, uw, ws)
            o_cross = each(lambda y, s: _dot(y[rows], s), q_dec, state)
            o_intra = each(lambda y, vn: _dot(y[rows, cc * c:(cc + 1) * c], vn), qk, v_new)
            kv = each(_dot_tn, k_dec, v_new)
            for hh, s, gl, d in zip(hs, state, g_last, kv):
                s_ref[hh] = s * jnp.exp(gl) + d
            for sl, oc, oi in zip(sls, o_cross, o_intra):
                o = oc + oi
                o_ref[rows, sl] = (_rms(o) * gn * _silu(gz_ref[rows, sl])).astype(BF16)


def _gated_deltanet(z, zs, zst, conv_w, a_log, dt_bias, out_norm, b, t):
    n = b * t
    tb, c = GDN_BLOCK, GDN_CHUNK
    nt = t // tb
    idx = np.arange(tb)
    same = (idx[:, None] // c) == (idx[None, :] // c)
    cum = (same & (idx[:, None] >= idx[None, :])).astype(np.float32)
    cum, cum_t = jnp.asarray(cum, BF16), jnp.asarray(cum.T, BF16)
    lanes = lambda a1: jnp.pad(a1, (HEADS, LANES - 2 * HEADS))[None, :]
    rows = lambda a1: jnp.broadcast_to(a1[:, None], (HEADS, tb))

    def col(off):
        return pl.BlockSpec((tb, WIDTH), lambda bi, i, off=off: (bi * nt + i, off // WIDTH))

    def ccol(blk):
        return pl.BlockSpec((CONV_WIDTH, WIDTH), lambda bi, i, blk=blk: (0, blk))

    const = lambda shape: pl.BlockSpec(shape, lambda bi, i: (0,) * len(shape))
    return pl.pallas_call(
        _gdn_kernel,
        grid=(b, nt),
        in_specs=[col(COL_GQ), col(COL_GK), col(COL_GV), col(COL_GZ), ccol(0), ccol(1), ccol(2),
                  pl.BlockSpec((tb, LANES), lambda bi, i: (bi * nt + i, 0)),
                  pl.BlockSpec((HEADS, tb), lambda bi, i: (1, bi * nt + i)),
                  const((1, LANES)), const((1, LANES)), const((HEADS, tb)), const((HEADS, tb)),
                  const((tb, tb)), const((tb, tb)), const((1, HEAD_DIM))],
        out_specs=pl.BlockSpec((tb, WIDTH), lambda bi, i: (bi * nt + i, 0)),
        out_shape=jax.ShapeDtypeStruct((n, WIDTH), BF16),
        scratch_shapes=[pltpu.VMEM((tb + SUBLANES, WIDTH), F32)] * 3 + [pltpu.VMEM((HEADS, HEAD_DIM, HEAD_DIM), F32)],
        compiler_params=_params("parallel", "arbitrary"),
        name="gated_deltanet",
    )(z, z, z, z, conv_w, conv_w, conv_w, zs, zst, lanes(a_log), lanes(dt_bias), rows(a_log), rows(dt_bias),
      cum, cum_t, out_norm)


def _merge_kernel(or_ref, os_ref, og_ref, wb_ref, gr_ref, gs_ref, gg_ref, o_ref):
    acc = jax.nn.sigmoid(gr_ref[...]) * jnp.dot(or_ref[...], wb_ref[0], preferred_element_type=F32)
    acc = acc + jax.nn.sigmoid(gs_ref[...]) * jnp.dot(os_ref[...], wb_ref[1], preferred_element_type=F32)
    acc = acc + jax.nn.sigmoid(gg_ref[...]) * jnp.dot(og_ref[...], wb_ref[2], preferred_element_type=F32)
    o_ref[...] = acc.astype(BF16)


def _merge(o_r, o_s, o_g, w_branch, z, d, tm, tn):
    n = o_r.shape[0]
    assert COL_GATES % tn == 0 and d % tn == 0
    branch = pl.BlockSpec((tm, WIDTH), lambda j, i: (i, 0))
    gate = lambda bidx: pl.BlockSpec((tm, tn), lambda j, i, bidx=bidx: (i, (COL_GATES + bidx * d) // tn + j))
    return pl.pallas_call(
        _merge_kernel,
        grid=(d // tn, n // tm),
        in_specs=[branch, branch, branch,
                  pl.BlockSpec((3, WIDTH, tn), lambda j, i: (0, 0, j)),
                  gate(0), gate(1), gate(2)],
        out_specs=pl.BlockSpec((tm, tn), lambda j, i: (i, j)),
        out_shape=jax.ShapeDtypeStruct((n, d), BF16),
        compiler_params=_params("parallel", "parallel"),
        name="branch_merge",
    )(o_r, o_s, o_g, w_branch, z, z, z)


def _norm_kernel(x_ref, g_ref, h_ref):
    h_ref[...] = (_rms(x_ref[...]) * g_ref[...]).astype(h_ref.dtype)


def _norm(x2, gain, tm):
    n, d = x2.shape
    return pl.pallas_call(
        _norm_kernel,
        grid=(n // tm,),
        in_specs=[pl.BlockSpec((tm, d), lambda i: (i, 0)), pl.BlockSpec((1, d), lambda i: (0, 0))],
        out_specs=pl.BlockSpec((tm, d), lambda i: (i, 0)),
        out_shape=jax.ShapeDtypeStruct((n, d), BF16),
        compiler_params=_params("parallel"),
        name="norm",
    )(x2, gain)


def _norm_router_kernel(x_ref, g_ref, wr_ref, h_ref, rt_ref):
    h = _rms(x_ref[...]) * g_ref[...]
    h_ref[...] = h
    logits = jnp.dot(h, wr_ref[...], preferred_element_type=F32, precision=lax.Precision.HIGHEST)
    lane = lax.broadcasted_iota(jnp.int32, logits.shape, 1)
    neg = jnp.float32(-jnp.inf)
    lg = jnp.where(lane < N_EXPERTS, logits, neg)
    m1 = jnp.max(lg, axis=-1, keepdims=True)
    i1 = jnp.min(jnp.where(lg == m1, lane, LANES), axis=-1, keepdims=True)
    lg2 = jnp.where(lane == i1, neg, lg)
    m2 = jnp.max(lg2, axis=-1, keepdims=True)
    i2 = jnp.min(jnp.where(lg2 == m2, lane, LANES), axis=-1, keepdims=True)
    e = jnp.exp(m2 - m1)
    g1 = 1.0 / (1.0 + e)
    g2 = e / (1.0 + e)
    rt_ref[...] = jnp.where(lane == 0, i1.astype(F32),
                            jnp.where(lane == 1, i2.astype(F32),
                                      jnp.where(lane == 2, g1, jnp.where(lane == 3, g2, 0.0))))


def _norm_router(x2, gain, w_router_pad, tm):
    n, d = x2.shape
    return pl.pallas_call(
        _norm_router_kernel,
        grid=(n // tm,),
        in_specs=[pl.BlockSpec((tm, d), lambda i: (i, 0)), pl.BlockSpec((1, d), lambda i: (0, 0)),
                  pl.BlockSpec((d, LANES), lambda i: (0, 0))],
        out_specs=[pl.BlockSpec((tm, d), lambda i: (i, 0)), pl.BlockSpec((tm, LANES), lambda i: (i, 0))],
        out_shape=[jax.ShapeDtypeStruct((n, d), F32), jax.ShapeDtypeStruct((n, LANES), F32)],
        compiler_params=_params("parallel"),
        name="norm_router",
    )(x2, gain, w_router_pad)


def _swiglu_kernel(be_ref, nu_ref, x_ref, wg_ref, wu_ref, wd_ref, *rest, residual):
    if residual:
        r_ref, o_ref, xb_ref = rest
    else:
        o_ref, xb_ref = rest
    blk, f = pl.program_id(0), pl.program_id(1)

    @pl.when(blk < nu_ref[0])
    def _():
        @pl.when(f == 0)
        def _():
            xb_ref[...] = x_ref[...].astype(BF16)
            o_ref[...] = r_ref[...] if residual else jnp.zeros_like(o_ref)

        xb = xb_ref[...]
        gate = jnp.dot(xb, wg_ref[0], preferred_element_type=F32)
        up = jnp.dot(xb, wu_ref[0], preferred_element_type=F32)
        act = (_silu(gate) * up).astype(BF16)
        o_ref[...] += jnp.dot(act, wd_ref[0], preferred_element_type=F32)

    @pl.when((blk >= nu_ref[0]) & (f == 0))
    def _():
        o_ref[...] = jnp.zeros_like(o_ref)


def _grouped_swiglu(x, w_gate, w_up, w_down, block_expert, n_used, bm, tf, residual=None):
    n, d = x.shape
    d_ff = w_gate.shape[2]
    nb, nf = n // bm, d_ff // tf

    def row(blk, f, be, nu):
        return (jnp.minimum(blk, nu[0] - 1), 0)

    def f_idx(blk, f, nu):
        return jnp.where(blk < nu[0], f, nf - 1)

    def w_in_map(blk, f, be, nu):
        return (be[jnp.minimum(blk, nu[0] - 1)], 0, f_idx(blk, f, nu))

    def w_out_map(blk, f, be, nu):
        return (be[jnp.minimum(blk, nu[0] - 1)], f_idx(blk, f, nu), 0)

    in_specs = [pl.BlockSpec((bm, d), row),
                pl.BlockSpec((1, d, tf), w_in_map), pl.BlockSpec((1, d, tf), w_in_map),
                pl.BlockSpec((1, tf, d), w_out_map)]
    args = [x, w_gate, w_up, w_down]
    if residual is not None:
        in_specs.append(pl.BlockSpec((bm, d), row))
        args.append(residual)
    return pl.pallas_call(
        functools.partial(_swiglu_kernel, residual=residual is not None),
        grid_spec=pltpu.PrefetchScalarGridSpec(
            num_scalar_prefetch=2, grid=(nb, nf), in_specs=in_specs,
            out_specs=pl.BlockSpec((bm, d), lambda blk, f, be, nu: (blk, 0)),
            scratch_shapes=[pltpu.VMEM((bm, d), BF16)]),
        out_shape=jax.ShapeDtypeStruct((n, d), F32),
        compiler_params=_params("arbitrary", "arbitrary"),
        name="grouped_swiglu",
    )(block_expert, n_used, *args)


def _row_copy(src_hbm, dst_ref, src_row, dst_row, sem):
    return pltpu.make_async_copy(src_hbm.at[pl.ds(src_row, 1), :], dst_ref.at[pl.ds(dst_row, 1), :], sem)


def _moe_swiglu_kernel(be_ref, nu_ref, tok_ref, h_hbm, wg_ref, wu_ref, wd_ref, o_ref, xg_ref, xb_ref, sem_ref,
                       *, bm, gather_steps):
    blk, f = pl.program_id(0), pl.program_id(1)
    n_used = nu_ref[0]
    slot = blk % 2
    rows_per_step = bm // gather_steps

    def fetch(block, row, to_slot):
        return _row_copy(h_hbm, xg_ref.at[to_slot], tok_ref[block * bm + row], row, sem_ref.at[to_slot])

    @pl.when((blk == 0) & (f == 0))
    def _():
        def start(r, carry):
            fetch(0, r, 0).start()
            return carry
        lax.fori_loop(0, bm, start, 0)

    @pl.when(blk < n_used)
    def _():
        @pl.when(f == 0)
        def _():
            def wait(r, carry):
                _row_copy(h_hbm, xg_ref.at[slot], 0, r, sem_ref.at[slot]).wait()
                return carry
            lax.fori_loop(0, bm, wait, 0)
            xb_ref[...] = xg_ref[slot].astype(BF16)
            o_ref[...] = jnp.zeros_like(o_ref)

        @pl.when((f < gather_steps) & (blk + 1 < n_used))
        def _():
            for r in range(rows_per_step):
                fetch(blk + 1, f * rows_per_step + r, 1 - slot).start()

        xb = xb_ref[...]
        gate = jnp.dot(xb, wg_ref[0], preferred_element_type=F32)
        up = jnp.dot(xb, wu_ref[0], preferred_element_type=F32)
        act = (_silu(gate) * up).astype(BF16)
        o_ref[...] += jnp.dot(act, wd_ref[0], preferred_element_type=F32)

    @pl.when((blk >= n_used) & (f == 0))
    def _():
        o_ref[...] = jnp.zeros_like(o_ref)


def _moe_swiglu(h, slot_tok, w_gate, w_up, w_down, block_expert, n_used, bm, tf):
    d = h.shape[1]
    d_ff = w_gate.shape[2]
    nb, nf = slot_tok.shape[0] // bm, d_ff // tf
    gather_steps = max(s for s in (1, 2, 4, MOE_GATHER_STEPS) if s <= nf)
    assert bm % gather_steps == 0

    def f_idx(blk, f, nu):
        return jnp.where(blk < nu[0], f, nf - 1)

    def w_in_map(blk, f, be, nu, tok):
        return (be[jnp.minimum(blk, nu[0] - 1)], 0, f_idx(blk, f, nu))

    def w_out_map(blk, f, be, nu, tok):
        return (be[jnp.minimum(blk, nu[0] - 1)], f_idx(blk, f, nu), 0)

    return pl.pallas_call(
        functools.partial(_moe_swiglu_kernel, bm=bm, gather_steps=gather_steps),
        grid_spec=pltpu.PrefetchScalarGridSpec(
            num_scalar_prefetch=3, grid=(nb, nf),
            in_specs=[pl.BlockSpec(memory_space=pl.ANY),
                      pl.BlockSpec((1, d, tf), w_in_map), pl.BlockSpec((1, d, tf), w_in_map),
                      pl.BlockSpec((1, tf, d), w_out_map)],
            out_specs=pl.BlockSpec((bm, d), lambda blk, f, be, nu, tok: (blk, 0)),
            scratch_shapes=[pltpu.VMEM((2, bm, d), F32), pltpu.VMEM((bm, d), BF16),
                            pltpu.SemaphoreType.DMA((2,))]),
        out_shape=jax.ShapeDtypeStruct((nb * bm, d), F32),
        compiler_params=_params("arbitrary", "arbitrary"),
        name="moe_swiglu",
    )(block_expert, n_used, slot_tok, h, w_gate, w_up, w_down)


def _combine_kernel(pos_ref, x_ref, rt_ref, y_hbm, o_ref, b0_ref, b1_ref, sem, *, tm):
    base = pl.program_id(0) * tm

    def start(r, carry):
        _row_copy(y_hbm, b0_ref, pos_ref[2 * (base + r)], r, sem).start()
        _row_copy(y_hbm, b1_ref, pos_ref[2 * (base + r) + 1], r, sem).start()
        return carry

    def wait(r, carry):
        _row_copy(y_hbm, b0_ref, 0, r, sem).wait()
        _row_copy(y_hbm, b1_ref, 0, r, sem).wait()
        return carry

    lax.fori_loop(0, tm, start, 0)
    lax.fori_loop(0, tm, wait, 0)
    rt = rt_ref[...]
    o_ref[...] = x_ref[...] + (b0_ref[...] * rt[:, 2:3] + b1_ref[...] * rt[:, 3:4])


def _combine(x2, route, y_sorted, pos, tm):
    n, d = x2.shape
    return pl.pallas_call(
        functools.partial(_combine_kernel, tm=tm),
        grid_spec=pltpu.PrefetchScalarGridSpec(
            num_scalar_prefetch=1, grid=(n // tm,),
            in_specs=[pl.BlockSpec((tm, d), lambda i, p: (i, 0)),
                      pl.BlockSpec((tm, LANES), lambda i, p: (i, 0)),
                      pl.BlockSpec(memory_space=pl.ANY)],
            out_specs=pl.BlockSpec((tm, d), lambda i, p: (i, 0)),
            scratch_shapes=[pltpu.VMEM((tm, d), F32), pltpu.VMEM((tm, d), F32), pltpu.SemaphoreType.DMA(())]),
        out_shape=jax.ShapeDtypeStruct((n, d), F32),
        compiler_params=_params("arbitrary"),
        name="combine",
    )(pos, x2, route, y_sorted)


def _tile(n, want):
    for cand in range(min(n, want), 0, -LANES):
        if n % cand == 0:
            return cand
    raise ValueError((n, want))


def _mixer(x2, b, t, norm_w, w_in, sb_qn, sb_kn, conv_w, a_log, dt_bias, gdn_on, w_branch, w_out):
    n, d = x2.shape
    w_main = jnp.concatenate([w_in[:, :SMALL_OFF], w_in[:, SMALL_OFF + 2 * HEADS:]], axis=1).astype(BF16)
    w_small = w_in[:, SMALL_OFF:SMALL_OFF + 2 * HEADS]
    w_small_pad = jnp.pad(w_small, ((0, 0), (0, LANES - 2 * HEADS))).astype(BF16)
    hidden, zs, zst = _norm_small(x2, norm_w[None, :], w_small_pad, w_small.T.astype(BF16), _tile(n, 512))
    z = _matmul(hidden, w_main, _tile(n, 2048), _tile(w_main.shape[1], 1024), F32)
    o_r = _retention(z, b, t, _tile(t, 512))
    o_s = _stick_breaking(z, sb_qn[None, :], sb_kn[None, :], b, t)
    o_g = _gated_deltanet(z, zs, zst, conv_w, a_log, dt_bias, gdn_on[None, :], b, t)
    merged = _merge(o_r, o_s, o_g, w_branch.astype(BF16), z, d, _tile(n, 512), _tile(d, 1024))
    return _matmul_residual(merged, w_out.astype(BF16), x2, _tile(n, 1024), _tile(d, 1024))


def _dense_ffn(x2, norm_w, w_gate, w_up, w_down, bm, tf):
    n = x2.shape[0]
    h = _norm(x2, norm_w[None, :], _tile(n, 512))
    nb = n // bm
    return _grouped_swiglu(h, w_gate[None].astype(BF16), w_up[None].astype(BF16), w_down[None].astype(BF16),
                           jnp.zeros((nb,), jnp.int32), jnp.full((1,), nb, jnp.int32), bm, tf, residual=x2)


def _moe_ffn(x2, norm_w, w_router, w_gate, w_up, w_down, bm, tf):
    n, d = x2.shape
    h, route = _norm_router(x2, norm_w[None, :], jnp.pad(w_router, ((0, 0), (0, LANES - N_EXPERTS))),
                            _tile(n, 512))
    flat_e = route[:, :TOP_K].astype(jnp.int32).reshape(-1)
    n_assign = n * TOP_K
    onehot = (flat_e[:, None] == jnp.arange(N_EXPERTS, dtype=jnp.int32)[None, :]).astype(jnp.int32)
    running = jnp.cumsum(onehot, axis=0)
    counts = running[-1]
    rank = jnp.sum((running - onehot) * onehot, axis=1)
    padded = (counts + bm - 1) // bm * bm
    pad_end = jnp.cumsum(padded)
    pad_start = pad_end - padded
    dest = (pad_start[flat_e] + rank).astype(jnp.int32)
    n_blocks = -(-n_assign // bm) + N_EXPERTS
    n_slots = n_blocks * bm
    flat_tok = jnp.arange(n_assign, dtype=jnp.int32) // TOP_K
    slot_tok = jnp.zeros((n_slots,), jnp.int32).at[dest].set(flat_tok)
    block_e = jnp.minimum(jnp.searchsorted(pad_end, jnp.arange(n_blocks, dtype=jnp.int32) * bm, side='right'),
                          N_EXPERTS - 1).astype(jnp.int32)
    n_used = (pad_end[-1:] // bm).astype(jnp.int32)
    y_sorted = _moe_swiglu(h, slot_tok, w_gate.astype(BF16), w_up.astype(BF16), w_down.astype(BF16),
                           block_e, n_used, bm, tf)
    return _combine(x2, route, y_sorted, dest, _tile(n, 256))


def kernel(x, mix_norm, w_in, sb_q_norm, sb_k_norm, gdn_conv, gdn_a_log, gdn_dt_bias, gdn_out_norm, w_branch,
           w_out, ffn_norm, w_ffn_gate, w_ffn_up, w_ffn_down, w_router, w_exp_gate, w_exp_up, w_exp_down):
    b, t, d = x.shape
    depth = mix_norm.shape[0]
    d_ff = w_ffn_gate.shape[2]
    x2 = x.reshape(b * t, d)
    bm = _tile(b * t, 512)
    tf = _tile(d_ff, 512)
    for layer in range(depth):
        x2 = _mixer(x2, b, t, mix_norm[layer], w_in[layer], sb_q_norm[layer], sb_k_norm[layer], gdn_conv[layer],
                    gdn_a_log[layer], gdn_dt_bias[layer], gdn_out_norm[layer], w_branch[layer], w_out[layer])
        i = layer // 2
        if layer % 2 == 0:
            x2 = _dense_ffn(x2, ffn_norm[layer], w_ffn_gate[i], w_ffn_up[i], w_ffn_down[i], bm, tf)
        else:
            x2 = _moe_ffn(x2, ffn_norm[layer], w_router[i], w_exp_gate[i], w_exp_up[i], w_exp_down[i], bm, tf)
    return x2.reshape(b, t, d)
```

```python
import functools

import numpy as np
import jax
import jax.numpy as jnp
from jax import lax
from jax.experimental import pallas as pl
from jax.experimental.pallas import tpu as pltpu

F32 = jnp.float32
BF16 = jnp.bfloat16

HEAD_DIM = 128
HEADS = 8
WIDTH = HEADS * HEAD_DIM
RET_CHUNK = 128
RET_HEAD_GROUP = 4
SB_TILE = 256
SB_HEADS = 2
SB_DEAD_CARRY = -104.0
GDN_CHUNK = 64
GDN_BLOCK = 256
GDN_HEAD_GROUP = 4
CONV_WIDTH = 4
ROPE_BASE = 10000.0
N_EXPERTS = 8
TOP_K = 2
EPS = 1e-6
LANES = 128
SUBLANES = 8
VMEM_LIMIT = 56 * 1024 * 1024

COL_RQ, COL_RK, COL_RV, COL_RG = 0, WIDTH, 2 * WIDTH, 3 * WIDTH
COL_SQ, COL_SK, COL_SV = 4 * WIDTH, 5 * WIDTH, 6 * WIDTH
COL_GQ, COL_GK, COL_GV, COL_GZ = 7 * WIDTH, 8 * WIDTH, 9 * WIDTH, 10 * WIDTH
COL_GATES = 11 * WIDTH
SMALL_OFF = 10 * WIDTH

NT = (((1,), (1,)), ((), ()))
TN = (((0,), (0,)), ((), ()))


def _params(*sem):
    return pltpu.CompilerParams(dimension_semantics=sem, vmem_limit_bytes=VMEM_LIMIT)


def _dot(a, b):
    return jnp.dot(a.astype(BF16), b.astype(BF16), preferred_element_type=F32)


def _dot_nt(a, b):
    return lax.dot_general(a.astype(BF16), b.astype(BF16), NT, preferred_element_type=F32)


def _dot_tn(a, b):
    return lax.dot_general(a.astype(BF16), b.astype(BF16), TN, preferred_element_type=F32)


def _split3(a):
    hi = a.astype(BF16)
    r1 = a - hi.astype(F32)
    mid = r1.astype(BF16)
    lo = (r1 - mid.astype(F32)).astype(BF16)
    return hi, mid, lo


def _dot_exact_rhs(a, b01):
    return sum(jnp.dot(t, b01, preferred_element_type=F32) for t in _split3(a))


def _dot_exact_lhs(a01, b):
    return sum(jnp.dot(a01, t, preferred_element_type=F32) for t in _split3(b))


def _silu(x):
    return x * jax.nn.sigmoid(x)


def _rms(x):
    return x * lax.rsqrt(jnp.mean(x * x, axis=-1, keepdims=True) + EPS)


def _norm_small_kernel(x_ref, g_ref, ws_ref, wst_ref, h_ref, zs_ref, zst_ref):
    hb = (_rms(x_ref[...]) * g_ref[...]).astype(BF16)
    h_ref[...] = hb
    zs_ref[...] = jnp.dot(hb, ws_ref[...], preferred_element_type=F32)
    zst_ref[...] = lax.dot_general(wst_ref[...], hb, NT, preferred_element_type=F32)


def _norm_small(x2, gain, w_small, w_small_t, tm):
    n, d = x2.shape
    return pl.pallas_call(
        _norm_small_kernel,
        grid=(n // tm,),
        in_specs=[pl.BlockSpec((tm, d), lambda i: (i, 0)),
                  pl.BlockSpec((1, d), lambda i: (0, 0)),
                  pl.BlockSpec((d, LANES), lambda i: (0, 0)),
                  pl.BlockSpec((2 * HEADS, d), lambda i: (0, 0))],
        out_specs=[pl.BlockSpec((tm, d), lambda i: (i, 0)),
                   pl.BlockSpec((tm, LANES), lambda i: (i, 0)),
                   pl.BlockSpec((2 * HEADS, tm), lambda i: (0, i))],
        out_shape=[jax.ShapeDtypeStruct((n, d), BF16),
                   jax.ShapeDtypeStruct((n, LANES), F32),
                   jax.ShapeDtypeStruct((2 * HEADS, n), F32)],
        compiler_params=_params("parallel"),
        name="norm_small",
    )(x2, gain, w_small, w_small_t)


def _mm_kernel(a_ref, w_ref, o_ref):
    o_ref[...] = jnp.dot(a_ref[...], w_ref[...], preferred_element_type=F32).astype(o_ref.dtype)


def _matmul(a, w, tm, tn, out_dtype):
    m, k = a.shape
    n = w.shape[1]
    return pl.pallas_call(
        _mm_kernel,
        grid=(m // tm, n // tn),
        in_specs=[pl.BlockSpec((tm, k), lambda i, j: (i, 0)),
                  pl.BlockSpec((k, tn), lambda i, j: (0, j))],
        out_specs=pl.BlockSpec((tm, tn), lambda i, j: (i, j)),
        out_shape=jax.ShapeDtypeStruct((m, n), out_dtype),
        compiler_params=_params("parallel", "arbitrary"),
        name="matmul",
    )(a, w)


def _mm_res_kernel(a_ref, w_ref, r_ref, o_ref):
    o_ref[...] = r_ref[...] + jnp.dot(a_ref[...], w_ref[...], preferred_element_type=F32)


def _matmul_residual(a, w, res, tm, tn):
    m, k = a.shape
    n = w.shape[1]
    return pl.pallas_call(
        _mm_res_kernel,
        grid=(m // tm, n // tn),
        in_specs=[pl.BlockSpec((tm, k), lambda i, j: (i, 0)),
                  pl.BlockSpec((k, tn), lambda i, j: (0, j)),
                  pl.BlockSpec((tm, tn), lambda i, j: (i, j))],
        out_specs=pl.BlockSpec((tm, tn), lambda i, j: (i, j)),
        out_shape=jax.ShapeDtypeStruct((m, n), F32),
        compiler_params=_params("parallel", "arbitrary"),
        name="matmul_residual",
    )(a, w, res)


def _ret_kernel(q_ref, k_ref, v_ref, g_ref, cos_ref, sin_ref, inner_ref, qh_ref, kt_ref, cd_ref,
                o_ref, s_ref, *, n_chunks):
    @pl.when(pl.program_id(1) == 0)
    def _():
        s_ref[...] = jnp.zeros_like(s_ref)

    scale = HEAD_DIM ** -0.5
    c = RET_CHUNK
    each = lambda fn, *lists: [fn(*args) for args in zip(*lists)]
    rot = lambda x, cos, sin: x * cos + pltpu.roll(x, HEAD_DIM // 2, 1) * sin
    for ci in range(n_chunks):
        rows = slice(ci * c, (ci + 1) * c)
        cos, sin = cos_ref[rows, :], sin_ref[rows, :]
        for h0 in range(0, HEADS, RET_HEAD_GROUP):
            hs = list(range(h0, h0 + RET_HEAD_GROUP))
            sls = [slice(hh * HEAD_DIM, (hh + 1) * HEAD_DIM) for hh in hs]
            qr = [rot(q_ref[rows, sl], cos, sin) for sl in sls]
            kr = [rot(k_ref[rows, sl], cos, sin) * scale for sl in sls]
            vb = [v_ref[rows, sl].astype(BF16) for sl in sls]
            state = [s_ref[hh] for hh in hs]
            scores = each(lambda a, b, hh: _dot_nt(a, b) * inner_ref[hh], qr, kr, hs)
            o_inner = each(_dot, scores, vb)
            o_cross = each(lambda a, s, hh: _dot(a * qh_ref[hh], s), qr, state, hs)
            kv = each(lambda a, b, hh: _dot_tn(a * kt_ref[hh], b), kr, vb, hs)
            for hh, s, d in zip(hs, state, kv):
                s_ref[hh] = s * cd_ref[hh][0:1, :] + d
            for sl, a, b in zip(sls, o_inner, o_cross):
                o_ref[rows, sl] = (_rms(a + b) * _silu(g_ref[rows, sl])).astype(BF16)


def _retention_tables(t):
    h = np.arange(HEADS, dtype=np.float64)
    log_gamma = np.log1p(-np.exp2(-5.0 - h))
    pos = np.arange(RET_CHUNK, dtype=np.float64)
    rel = pos[:, None] - pos[None, :]
    inner = np.where(rel >= 0, np.exp(np.maximum(rel, 0.0)[None] * log_gamma[:, None, None]), 0.0)
    k_tail = np.exp((RET_CHUNK - 1.0 - pos)[None, :] * log_gamma[:, None])
    q_head = np.exp((pos + 1.0)[None, :] * log_gamma[:, None])
    chunk_decay = np.exp(RET_CHUNK * log_gamma)
    half = HEAD_DIM // 2
    inv_freq = ROPE_BASE ** (-np.arange(half, dtype=np.float64) / half)
    ang = np.arange(t, dtype=np.float64)[:, None] * inv_freq[None, :]
    cos = np.concatenate([np.cos(ang), np.cos(ang)], axis=1)
    sin = np.concatenate([-np.sin(ang), np.sin(ang)], axis=1)
    bc = lambda a: np.broadcast_to(a[:, :, None], (HEADS, RET_CHUNK, HEAD_DIM))
    f = lambda a: jnp.asarray(np.ascontiguousarray(a), F32)
    return (f(cos), f(sin), f(inner), f(bc(q_head)), f(bc(k_tail)),
            f(np.broadcast_to(chunk_decay[:, None, None], (HEADS, SUBLANES, HEAD_DIM))))


def _retention(z, b, t, tb):
    n = b * t
    nt = t // tb
    cos, sin, inner, q_head, k_tail, chunk_decay = _retention_tables(t)
    col = lambda off: pl.BlockSpec((tb, WIDTH), lambda bi, i, off=off: (bi * nt + i, off // WIDTH))
    pos = pl.BlockSpec((tb, HEAD_DIM), lambda bi, i: (i, 0))
    head_tab = lambda r: pl.BlockSpec((HEADS, r, HEAD_DIM), lambda bi, i: (0, 0, 0))
    return pl.pallas_call(
        functools.partial(_ret_kernel, n_chunks=tb // RET_CHUNK),
        grid=(b, nt),
        in_specs=[col(COL_RQ), col(COL_RK), col(COL_RV), col(COL_RG), pos, pos,
                  head_tab(RET_CHUNK), head_tab(RET_CHUNK), head_tab(RET_CHUNK), head_tab(SUBLANES)],
        out_specs=pl.BlockSpec((tb, WIDTH), lambda bi, i: (bi * nt + i, 0)),
        out_shape=jax.ShapeDtypeStruct((n, WIDTH), BF16),
        scratch_shapes=[pltpu.VMEM((HEADS, HEAD_DIM, HEAD_DIM), F32)],
        compiler_params=_params("parallel", "arbitrary"),
        name="retention",
    )(z, z, z, z, cos, sin, inner, q_head, k_tail, chunk_decay)


def _split2(a):
    hi = a.astype(BF16)
    return hi, (a - hi.astype(F32)).astype(BF16)


def _sb_kernel(q_ref, k_ref, v_ref, qn_ref, kn_ref, u_ref, o_ref, kb_ref, vb_ref, acc_ref, carry_ref, *, prep_rows):
    i = pl.program_id(2)
    t = k_ref.shape[0]
    bq = q_ref.shape[0]
    heads = [slice(h * HEAD_DIM, (h + 1) * HEAD_DIM) for h in range(SB_HEADS)]

    @pl.when(i == 0)
    def _():
        def prep(c, carry):
            rows = pl.ds(pl.multiple_of(c * prep_rows, prep_rows), prep_rows)
            for sl in heads:
                kb_ref[rows, sl] = (_rms(k_ref[rows, sl]) * kn_ref[...]).astype(BF16)
            vb_ref[rows, :] = v_ref[rows, :].astype(BF16)
            return carry
        lax.fori_loop(0, t // prep_rows, prep, 0)

    qb = [(_rms(q_ref[:, sl]) * qn_ref[...] * HEAD_DIM ** -0.5).astype(BF16) for sl in heads]
    u = u_ref[...]
    each = lambda fn, *lists: [fn(*args) for args in zip(*lists)]
    d32 = lambda a, b: jnp.dot(a, b, preferred_element_type=F32)
    streams = [(h, far) for h in range(SB_HEADS) for far in (0, 1)]

    def tile_pair(j, diagonal):
        has_far = j >= 1
        starts = [pl.multiple_of(j * bq, bq), pl.multiple_of(jnp.maximum(j - 1, 0) * bq, bq)]
        z = [lax.dot_general(qb[h], kb_ref[pl.ds(starts[far], bq), heads[h]], NT, preferred_element_type=F32)
             for h, far in streams]
        log_not = each(lambda x: -(jnp.maximum(x, 0.0) + jnp.log(1.0 + jnp.exp(-jnp.abs(x)))), z)
        if diagonal:
            causal = lax.broadcasted_iota(jnp.int32, (bq, bq), 1) < lax.broadcasted_iota(jnp.int32, (bq, bq), 0)
        for n, (h, far) in enumerate(streams):
            if far:
                log_not[n] = jnp.where(has_far, log_not[n], 0.0)
            elif diagonal:
                log_not[n] = jnp.where(causal, log_not[n], 0.0)
        split = each(_split2, log_not)
        later = [d32(hi, u) + d32(lo, u) for hi, lo in split]
        total = each(lambda x: jnp.sum(x, axis=1, keepdims=True), log_not)
        logw = each(lambda a, b, c: a + b + c, log_not, z, later)
        tops = []
        for h in range(SB_HEADS):
            near, far = 2 * h, 2 * h + 1
            if diagonal:
                w_near = jnp.where(causal, jnp.exp(logw[near]), 0.0)
                before_far = total[near]
            else:
                old = carry_ref[h]
                w_near = jnp.exp(logw[near] + old)
                before_far = old + total[near]
            w_far = jnp.where(has_far, jnp.exp(logw[far] + before_far), 0.0)
            pv = (d32(w_near.astype(BF16), vb_ref[pl.ds(starts[0], bq), heads[h]])
                  + d32(w_far.astype(BF16), vb_ref[pl.ds(starts[1], bq), heads[h]]))
            acc_ref[:, heads[h]] = pv if diagonal else acc_ref[:, heads[h]] + pv
            carry = before_far + total[far]
            carry_ref[h] = carry
            tops.append(jnp.max(carry))
        return functools.reduce(jnp.maximum, tops)

    def live(state):
        return (state[0] >= 0) & (state[1] > SB_DEAD_CARRY)

    def step(state):
        return state[0] - 2, tile_pair(state[0], False)

    lax.while_loop(live, step, (i - 2, tile_pair(i, True)))
    o_ref[...] = acc_ref[...].astype(BF16)


def _stick_breaking(z, qn, kn, b, t):
    n = b * t
    bq = _tile(t, SB_TILE)
    nt = t // bq
    w = SB_HEADS * HEAD_DIM
    j = np.arange(bq)
    u = jnp.asarray(j[:, None] > j[None, :], BF16)
    seq = lambda off: pl.BlockSpec((t, w), lambda bi, g, i, off=off: (bi, off // w + g))
    return pl.pallas_call(
        functools.partial(_sb_kernel, prep_rows=bq),
        grid=(b, HEADS // SB_HEADS, nt),
        in_specs=[pl.BlockSpec((bq, w), lambda bi, g, i: (bi * nt + i, COL_SQ // w + g)),
                  seq(COL_SK), seq(COL_SV),
                  pl.BlockSpec((1, HEAD_DIM), lambda bi, g, i: (0, 0)),
                  pl.BlockSpec((1, HEAD_DIM), lambda bi, g, i: (0, 0)),
                  pl.BlockSpec((bq, bq), lambda bi, g, i: (0, 0))],
        out_specs=pl.BlockSpec((bq, w), lambda bi, g, i: (bi * nt + i, g)),
        out_shape=jax.ShapeDtypeStruct((n, WIDTH), BF16),
        scratch_shapes=[pltpu.VMEM((t, w), BF16), pltpu.VMEM((t, w), BF16),
                        pltpu.VMEM((bq, w), F32), pltpu.VMEM((SB_HEADS, bq, 1), F32)],
        compiler_params=_params("parallel", "parallel", "arbitrary"),
        name="stick_breaking",
    )(z, z, z, qn, kn, u)


def _gdn_kernel(xq_ref, xk_ref, xv_ref, gz_ref, cq_ref, ck_ref, cv_ref, zs_ref, gat_ref, alog_ref, dt_ref,
                alog_t_ref, dt_t_ref, cum_ref, cum_t_ref, gn_ref, o_ref, bq_ref, bk_ref, bv_ref, s_ref):
    tb = GDN_BLOCK
    c = GDN_CHUNK
    halo = SUBLANES

    @pl.when(pl.program_id(1) == 0)
    def _():
        s_ref[...] = jnp.zeros_like(s_ref)
        for buf in (bq_ref, bk_ref, bv_ref):
            buf[0:halo, :] = jnp.zeros((halo, WIDTH), F32)

    def conv(x_ref, cw_ref, buf):
        buf[halo:halo + tb, :] = x_ref[...]
        y = cw_ref[CONV_WIDTH - 1:CONV_WIDTH, :] * buf[halo:halo + tb, :]
        for tap in range(CONV_WIDTH - 1):
            back = CONV_WIDTH - 1 - tap
            y = y + cw_ref[tap:tap + 1, :] * buf[halo - back:halo - back + tb, :]
        buf[0:halo, :] = buf[tb:tb + halo, :]
        buf[halo:halo + tb, :] = _silu(y)

    conv(xq_ref, cq_ref, bq_ref)
    conv(xk_ref, ck_ref, bk_ref)
    conv(xv_ref, cv_ref, bv_ref)

    zs = zs_ref[...]
    beta_lanes = jax.nn.sigmoid(zs)
    g_lanes = -jnp.exp(alog_ref[...]) * jax.nn.softplus(zs + dt_ref[...])
    gc_lanes = _dot_exact_lhs(cum_ref[...], g_lanes)
    g_t = -jnp.exp(alog_t_ref[...]) * jax.nn.softplus(gat_ref[...] + dt_t_ref[...])
    gc_t = _dot_exact_rhs(g_t, cum_t_ref[...])

    ri = lax.broadcasted_iota(jnp.int32, (tb, tb), 0)
    ci = lax.broadcasted_iota(jnp.int32, (tb, tb), 1)
    same = (ri // c) == (ci // c)
    lower = same & (ri >= ci)
    strict = same & (ri > ci)
    scale = HEAD_DIM ** -0.5
    gn = gn_ref[...]

    def each(fn, *lists):
        return [fn(*args) for args in zip(*lists)]

    def wide(m):
        return functools.reduce(lambda x, y: x + y, [m[cc * c:(cc + 1) * c, :] for cc in range(tb // c)])

    def block_diag(m):
        return jnp.where(same, jnp.concatenate([m] * (tb // c), axis=0), 0.0)

    eye_wide = (lax.broadcasted_iota(jnp.int32, (c, tb), 0) == lax.broadcasted_iota(jnp.int32, (c, tb), 1) % c).astype(F32)

    for h0 in range(0, HEADS, GDN_HEAD_GROUP):
        hs = list(range(h0, h0 + GDN_HEAD_GROUP))
        sls = [slice(hh * HEAD_DIM, (hh + 1) * HEAD_DIM) for hh in hs]
        q = [bq_ref[halo:halo + tb, sl] for sl in sls]
        k = [bk_ref[halo:halo + tb, sl] for sl in sls]
        v = [bv_ref[halo:halo + tb, sl] for sl in sls]
        q = each(lambda x: x * lax.rsqrt(jnp.sum(x * x, axis=-1, keepdims=True) + EPS) * scale, q)
        k = each(lambda x: x * lax.rsqrt(jnp.sum(x * x, axis=-1, keepdims=True) + EPS), k)
        beta = [beta_lanes[:, hh:hh + 1] for hh in hs]
        gc = [gc_lanes[:, HEADS + hh:HEADS + hh + 1] for hh in hs]
        gc_rows = [gc_t[hh:hh + 1, :] for hh in hs]
        decay = each(lambda gi, gj: jnp.where(lower, jnp.exp(jnp.where(lower, gi - gj, 0.0)), 0.0), gc, gc_rows)
        k_beta = each(lambda x, y: x * y, k, beta)
        kk = each(_dot_nt, k_beta, k)
        a = each(lambda x, d: jnp.where(strict, x * d, 0.0), kk, decay)
        x = each(lambda y: eye_wide - wide(y), a)
        p = each(lambda y: _dot(wide(y), y), a)
        for step in range(5):
            p_bd = each(block_diag, p)
            x = each(lambda y, d: y + _dot(y, d), x, p_bd)
            if step < 4:
                p = each(_dot, p, p_bd)
        t_inv = each(block_diag, x)
        e_gc = each(jnp.exp, gc)
        rhs = each(lambda vv, bb, kb, eg: jnp.concatenate([vv * bb, kb * eg], axis=1), v, beta, k_beta, e_gc)
        uw = each(_dot, t_inv, rhs)
        qk = each(lambda y, d: jnp.where(lower, y * d, 0.0), each(_dot_nt, q, k), decay)
        q_dec = each(lambda y, eg: y * eg, q, e_gc)
        for cc in range(tb // c):
            rows = slice(cc * c, (cc + 1) * c)
            g_last = [g[cc * c + c - 1:cc * c + c, :] for g in gc]
            k_dec = each(lambda kx, gl, g: kx[rows] * jnp.exp(gl - g[rows]), k, g_last, gc)
            state = [s_ref[hh] for hh in hs]
            ws = each(lambda y, s: _dot(y[rows, HEAD_DIM:], s), uw, state)
            v_new = each(lambda y, d: y[rows, :HEAD_DIM] - d, uw, ws)
            o_cross = each(lambda y, s: _dot(y[rows], s), q_dec, state)
            o_intra = each(lambda y, vn: _dot(y[rows, cc * c:(cc + 1) * c], vn), qk, v_new)
            kv = each(_dot_tn, k_dec, v_new)
            for hh, s, gl, d in zip(hs, state, g_last, kv):
                s_ref[hh] = s * jnp.exp(gl) + d
            for sl, oc, oi in zip(sls, o_cross, o_intra):
                o = oc + oi
                o_ref[rows, sl] = (_rms(o) * gn * _silu(gz_ref[rows, sl])).astype(BF16)


def _gated_deltanet(z, zs, zst, conv_w, a_log, dt_bias, out_norm, b, t):
    n = b * t
    tb, c = GDN_BLOCK, GDN_CHUNK
    nt = t // tb
    idx = np.arange(tb)
    same = (idx[:, None] // c) == (idx[None, :] // c)
    cum = (same & (idx[:, None] >= idx[None, :])).astype(np.float32)
    cum, cum_t = jnp.asarray(cum, BF16), jnp.asarray(cum.T, BF16)
    lanes = lambda a1: jnp.pad(a1, (HEADS, LANES - 2 * HEADS))[None, :]
    rows = lambda a1: jnp.broadcast_to(a1[:, None], (HEADS, tb))

    def col(off):
        return pl.BlockSpec((tb, WIDTH), lambda bi, i, off=off: (bi * nt + i, off // WIDTH))

    def ccol(blk):
        return pl.BlockSpec((CONV_WIDTH, WIDTH), lambda bi, i, blk=blk: (0, blk))

    const = lambda shape: pl.BlockSpec(shape, lambda bi, i: (0,) * len(shape))
    return pl.pallas_call(
        _gdn_kernel,
        grid=(b, nt),
        in_specs=[col(COL_GQ), col(COL_GK), col(COL_GV), col(COL_GZ), ccol(0), ccol(1), ccol(2),
                  pl.BlockSpec((tb, LANES), lambda bi, i: (bi * nt + i, 0)),
                  pl.BlockSpec((HEADS, tb), lambda bi, i: (1, bi * nt + i)),
                  const((1, LANES)), const((1, LANES)), const((HEADS, tb)), const((HEADS, tb)),
                  const((tb, tb)), const((tb, tb)), const((1, HEAD_DIM))],
        out_specs=pl.BlockSpec((tb, WIDTH), lambda bi, i: (bi * nt + i, 0)),
        out_shape=jax.ShapeDtypeStruct((n, WIDTH), BF16),
        scratch_shapes=[pltpu.VMEM((tb + SUBLANES, WIDTH), F32)] * 3 + [pltpu.VMEM((HEADS, HEAD_DIM, HEAD_DIM), F32)],
        compiler_params=_params("parallel", "arbitrary"),
        name="gated_deltanet",
    )(z, z, z, z, conv_w, conv_w, conv_w, zs, zst, lanes(a_log), lanes(dt_bias), rows(a_log), rows(dt_bias),
      cum, cum_t, out_norm)


def _merge_kernel(or_ref, os_ref, og_ref, wb_ref, gr_ref, gs_ref, gg_ref, o_ref):
    acc = jax.nn.sigmoid(gr_ref[...]) * jnp.dot(or_ref[...], wb_ref[0], preferred_element_type=F32)
    acc = acc + jax.nn.sigmoid(gs_ref[...]) * jnp.dot(os_ref[...], wb_ref[1], preferred_element_type=F32)
    acc = acc + jax.nn.sigmoid(gg_ref[...]) * jnp.dot(og_ref[...], wb_ref[2], preferred_element_type=F32)
    o_ref[...] = acc.astype(BF16)


def _merge(o_r, o_s, o_g, w_branch, z, d, tm, tn):
    n = o_r.shape[0]
    assert COL_GATES % tn == 0 and d % tn == 0
    branch = pl.BlockSpec((tm, WIDTH), lambda j, i: (i, 0))
    gate = lambda bidx: pl.BlockSpec((tm, tn), lambda j, i, bidx=bidx: (i, (COL_GATES + bidx * d) // tn + j))
    return pl.pallas_call(
        _merge_kernel,
        grid=(d // tn, n // tm),
        in_specs=[branch, branch, branch,
                  pl.BlockSpec((3, WIDTH, tn), lambda j, i: (0, 0, j)),
                  gate(0), gate(1), gate(2)],
        out_specs=pl.BlockSpec((tm, tn), lambda j, i: (i, j)),
        out_shape=jax.ShapeDtypeStruct((n, d), BF16),
        compiler_params=_params("parallel", "parallel"),
        name="branch_merge",
    )(o_r, o_s, o_g, w_branch, z, z, z)


def _norm_kernel(x_ref, g_ref, h_ref):
    h_ref[...] = (_rms(x_ref[...]) * g_ref[...]).astype(h_ref.dtype)


def _norm(x2, gain, tm):
    n, d = x2.shape
    return pl.pallas_call(
        _norm_kernel,
        grid=(n // tm,),
        in_specs=[pl.BlockSpec((tm, d), lambda i: (i, 0)), pl.BlockSpec((1, d), lambda i: (0, 0))],
        out_specs=pl.BlockSpec((tm, d), lambda i: (i, 0)),
        out_shape=jax.ShapeDtypeStruct((n, d), BF16),
        compiler_params=_params("parallel"),
        name="norm",
    )(x2, gain)


def _norm_router_kernel(x_ref, g_ref, wr_ref, h_ref, rt_ref):
    h = _rms(x_ref[...]) * g_ref[...]
    h_ref[...] = h
    logits = jnp.dot(h, wr_ref[...], preferred_element_type=F32, precision=lax.Precision.HIGHEST)
    lane = lax.broadcasted_iota(jnp.int32, logits.shape, 1)
    neg = jnp.float32(-jnp.inf)
    lg = jnp.where(lane < N_EXPERTS, logits, neg)
    m1 = jnp.max(lg, axis=-1, keepdims=True)
    i1 = jnp.min(jnp.where(lg == m1, lane, LANES), axis=-1, keepdims=True)
    lg2 = jnp.where(lane == i1, neg, lg)
    m2 = jnp.max(lg2, axis=-1, keepdims=True)
    i2 = jnp.min(jnp.where(lg2 == m2, lane, LANES), axis=-1, keepdims=True)
    e = jnp.exp(m2 - m1)
    g1 = 1.0 / (1.0 + e)
    g2 = e / (1.0 + e)
    rt_ref[...] = jnp.where(lane == 0, i1.astype(F32),
                            jnp.where(lane == 1, i2.astype(F32),
                                      jnp.where(lane == 2, g1, jnp.where(lane == 3, g2, 0.0))))


def _norm_router(x2, gain, w_router_pad, tm):
    n, d = x2.shape
    return pl.pallas_call(
        _norm_router_kernel,
        grid=(n // tm,),
        in_specs=[pl.BlockSpec((tm, d), lambda i: (i, 0)), pl.BlockSpec((1, d), lambda i: (0, 0)),
                  pl.BlockSpec((d, LANES), lambda i: (0, 0))],
        out_specs=[pl.BlockSpec((tm, d), lambda i: (i, 0)), pl.BlockSpec((tm, LANES), lambda i: (i, 0))],
        out_shape=[jax.ShapeDtypeStruct((n, d), F32), jax.ShapeDtypeStruct((n, LANES), F32)],
        compiler_params=_params("parallel"),
        name="norm_router",
    )(x2, gain, w_router_pad)


def _swiglu_kernel(be_ref, nu_ref, x_ref, wg_ref, wu_ref, wd_ref, *rest, residual):
    if residual:
        r_ref, o_ref, xb_ref = rest
    else:
        o_ref, xb_ref = rest
    blk, f = pl.program_id(0), pl.program_id(1)

    @pl.when(blk < nu_ref[0])
    def _():
        @pl.when(f == 0)
        def _():
            xb_ref[...] = x_ref[...].astype(BF16)
            o_ref[...] = r_ref[...] if residual else jnp.zeros_like(o_ref)

        xb = xb_ref[...]
        gate = jnp.dot(xb, wg_ref[0], preferred_element_type=F32)
        up = jnp.dot(xb, wu_ref[0], preferred_element_type=F32)
        act = (_silu(gate) * up).astype(BF16)
        o_ref[...] += jnp.dot(act, wd_ref[0], preferred_element_type=F32)

    @pl.when((blk >= nu_ref[0]) & (f == 0))
    def _():
        o_ref[...] = jnp.zeros_like(o_ref)


def _grouped_swiglu(x, w_gate, w_up, w_down, block_expert, n_used, bm, tf, residual=None):
    n, d = x.shape
    d_ff = w_gate.shape[2]
    nb, nf = n // bm, d_ff // tf

    def row(blk, f, be, nu):
        return (jnp.minimum(blk, nu[0] - 1), 0)

    def f_idx(blk, f, nu):
        return jnp.where(blk < nu[0], f, nf - 1)

    def w_in_map(blk, f, be, nu):
        return (be[jnp.minimum(blk, nu[0] - 1)], 0, f_idx(blk, f, nu))

    def w_out_map(blk, f, be, nu):
        return (be[jnp.minimum(blk, nu[0] - 1)], f_idx(blk, f, nu), 0)

    in_specs = [pl.BlockSpec((bm, d), row),
                pl.BlockSpec((1, d, tf), w_in_map), pl.BlockSpec((1, d, tf), w_in_map),
                pl.BlockSpec((1, tf, d), w_out_map)]
    args = [x, w_gate, w_up, w_down]
    if residual is not None:
        in_specs.append(pl.BlockSpec((bm, d), row))
        args.append(residual)
    return pl.pallas_call(
        functools.partial(_swiglu_kernel, residual=residual is not None),
        grid_spec=pltpu.PrefetchScalarGridSpec(
            num_scalar_prefetch=2, grid=(nb, nf), in_specs=in_specs,
            out_specs=pl.BlockSpec((bm, d), lambda blk, f, be, nu: (blk, 0)),
            scratch_shapes=[pltpu.VMEM((bm, d), BF16)]),
        out_shape=jax.ShapeDtypeStruct((n, d), F32),
        compiler_params=_params("arbitrary", "arbitrary"),
        name="grouped_swiglu",
    )(block_expert, n_used, *args)


def _row_copy(src_hbm, dst_ref, src_row, dst_row, sem):
    return pltpu.make_async_copy(src_hbm.at[pl.ds(src_row, 1), :], dst_ref.at[pl.ds(dst_row, 1), :], sem)


def _moe_swiglu_kernel(be_ref, nu_ref, tok_ref, h_hbm, wg_ref, wu_ref, wd_ref, o_ref, xg_ref, xb_ref, sem_ref,
                       *, bm, nf):
    blk, f = pl.program_id(0), pl.program_id(1)
    n_used = nu_ref[0]
    slot = blk % 2
    rows_per_step = -(-bm // nf)
    fetched = rows_per_step * nf

    def fetch(block, row, to_slot):
        return _row_copy(h_hbm, xg_ref.at[to_slot], tok_ref[block * bm + row], row, sem_ref.at[to_slot])

    def wait_fetched(of_slot):
        def wait(r, carry):
            _row_copy(h_hbm, xg_ref.at[of_slot], 0, r, sem_ref.at[of_slot]).wait()
            return carry
        lax.fori_loop(0, fetched, wait, 0)

    @pl.when((blk == 0) & (f == 0))
    def _():
        def start(r, carry):
            fetch(0, r, 0).start()
            return carry
        lax.fori_loop(0, fetched, start, 0)

    @pl.when(blk < n_used)
    def _():
        @pl.when(f == 0)
        def _():
            wait_fetched(slot)
            xb_ref[...] = xg_ref[slot, 0:bm, :].astype(BF16)
            o_ref[...] = jnp.zeros_like(o_ref)

        for r in range(rows_per_step):
            fetch(blk + 1, f * rows_per_step + r, 1 - slot).start()

        xb = xb_ref[...]
        gate = jnp.dot(xb, wg_ref[0], preferred_element_type=F32)
        up = jnp.dot(xb, wu_ref[0], preferred_element_type=F32)
        act = (_silu(gate) * up).astype(BF16)
        o_ref[...] += jnp.dot(act, wd_ref[0], preferred_element_type=F32)

    @pl.when((blk >= n_used) & (f == 0))
    def _():
        @pl.when(blk == n_used)
        def _():
            wait_fetched(slot)
        o_ref[...] = jnp.zeros_like(o_ref)


def _moe_swiglu(h, slot_tok, w_gate, w_up, w_down, block_expert, n_used, bm, tf):
    d = h.shape[1]
    d_ff = w_gate.shape[2]
    nb, nf = slot_tok.shape[0] // bm, d_ff // tf
    buffer_rows = -(-(-(-bm // nf) * nf) // SUBLANES) * SUBLANES

    def f_idx(blk, f, nu):
        return jnp.where(blk < nu[0], f, nf - 1)

    def w_in_map(blk, f, be, nu, tok):
        return (be[jnp.minimum(blk, nu[0] - 1)], 0, f_idx(blk, f, nu))

    def w_out_map(blk, f, be, nu, tok):
        return (be[jnp.minimum(blk, nu[0] - 1)], f_idx(blk, f, nu), 0)

    return pl.pallas_call(
        functools.partial(_moe_swiglu_kernel, bm=bm, nf=nf),
        grid_spec=pltpu.PrefetchScalarGridSpec(
            num_scalar_prefetch=3, grid=(nb, nf),
            in_specs=[pl.BlockSpec(memory_space=pl.ANY),
                      pl.BlockSpec((1, d, tf), w_in_map), pl.BlockSpec((1, d, tf), w_in_map),
                      pl.BlockSpec((1, tf, d), w_out_map)],
            out_specs=pl.BlockSpec((bm, d), lambda blk, f, be, nu, tok: (blk, 0)),
            scratch_shapes=[pltpu.VMEM((2, buffer_rows, d), F32), pltpu.VMEM((bm, d), BF16),
                            pltpu.SemaphoreType.DMA((2,))]),
        out_shape=jax.ShapeDtypeStruct((nb * bm, d), F32),
        compiler_params=_params("arbitrary", "arbitrary"),
        name="moe_swiglu",
    )(block_expert, n_used, slot_tok, h, w_gate, w_up, w_down)


def _combine_kernel(pos_ref, x_ref, rt_ref, y_hbm, o_ref, b0_ref, b1_ref, sem, *, tm):
    base = pl.program_id(0) * tm

    def start(r, carry):
        _row_copy(y_hbm, b0_ref, pos_ref[2 * (base + r)], r, sem).start()
        _row_copy(y_hbm, b1_ref, pos_ref[2 * (base + r) + 1], r, sem).start()
        return carry

    def wait(r, carry):
        _row_copy(y_hbm, b0_ref, 0, r, sem).wait()
        _row_copy(y_hbm, b1_ref, 0, r, sem).wait()
        return carry

    lax.fori_loop(0, tm, start, 0)
    lax.fori_loop(0, tm, wait, 0)
    rt = rt_ref[...]
    o_ref[...] = x_ref[...] + (b0_ref[...] * rt[:, 2:3] + b1_ref[...] * rt[:, 3:4])


def _combine(x2, route, y_sorted, pos, tm):
    n, d = x2.shape
    return pl.pallas_call(
        functools.partial(_combine_kernel, tm=tm),
        grid_spec=pltpu.PrefetchScalarGridSpec(
            num_scalar_prefetch=1, grid=(n // tm,),
            in_specs=[pl.BlockSpec((tm, d), lambda i, p: (i, 0)),
                      pl.BlockSpec((tm, LANES), lambda i, p: (i, 0)),
                      pl.BlockSpec(memory_space=pl.ANY)],
            out_specs=pl.BlockSpec((tm, d), lambda i, p: (i, 0)),
            scratch_shapes=[pltpu.VMEM((tm, d), F32), pltpu.VMEM((tm, d), F32), pltpu.SemaphoreType.DMA(())]),
        out_shape=jax.ShapeDtypeStruct((n, d), F32),
        compiler_params=_params("arbitrary"),
        name="combine",
    )(pos, x2, route, y_sorted)


def _tile(n, want):
    for cand in range(min(n, want), 0, -LANES):
        if n % cand == 0:
            return cand
    raise ValueError((n, want))


def _mixer(x2, b, t, norm_w, w_in, sb_qn, sb_kn, conv_w, a_log, dt_bias, gdn_on, w_branch, w_out):
    n, d = x2.shape
    w_main = jnp.concatenate([w_in[:, :SMALL_OFF], w_in[:, SMALL_OFF + 2 * HEADS:]], axis=1).astype(BF16)
    w_small = w_in[:, SMALL_OFF:SMALL_OFF + 2 * HEADS]
    w_small_pad = jnp.pad(w_small, ((0, 0), (0, LANES - 2 * HEADS))).astype(BF16)
    hidden, zs, zst = _norm_small(x2, norm_w[None, :], w_small_pad, w_small.T.astype(BF16), _tile(n, 512))
    z = _matmul(hidden, w_main, _tile(n, 2048), _tile(w_main.shape[1], 1024), F32)
    o_r = _retention(z, b, t, _tile(t, 512))
    o_s = _stick_breaking(z, sb_qn[None, :], sb_kn[None, :], b, t)
    o_g = _gated_deltanet(z, zs, zst, conv_w, a_log, dt_bias, gdn_on[None, :], b, t)
    merged = _merge(o_r, o_s, o_g, w_branch.astype(BF16), z, d, _tile(n, 512), _tile(d, 1024))
    return _matmul_residual(merged, w_out.astype(BF16), x2, _tile(n, 1024), _tile(d, 1024))


def _dense_ffn(x2, norm_w, w_gate, w_up, w_down, bm, tf):
    n = x2.shape[0]
    h = _norm(x2, norm_w[None, :], _tile(n, 512))
    nb = n // bm
    return _grouped_swiglu(h, w_gate[None].astype(BF16), w_up[None].astype(BF16), w_down[None].astype(BF16),
                           jnp.zeros((nb,), jnp.int32), jnp.full((1,), nb, jnp.int32), bm, tf, residual=x2)


def _moe_ffn(x2, norm_w, w_router, w_gate, w_up, w_down, bm, tf):
    n, d = x2.shape
    h, route = _norm_router(x2, norm_w[None, :], jnp.pad(w_router, ((0, 0), (0, LANES - N_EXPERTS))),
                            _tile(n, 512))
    flat_e = route[:, :TOP_K].astype(jnp.int32).reshape(-1)
    n_assign = n * TOP_K
    onehot = (flat_e[:, None] == jnp.arange(N_EXPERTS, dtype=jnp.int32)[None, :]).astype(jnp.int32)
    running = jnp.cumsum(onehot, axis=0)
    counts = running[-1]
    rank = jnp.sum((running - onehot) * onehot, axis=1)
    padded = (counts + bm - 1) // bm * bm
    pad_end = jnp.cumsum(padded)
    pad_start = pad_end - padded
    dest = (pad_start[flat_e] + rank).astype(jnp.int32)
    n_blocks = -(-n_assign // bm) + N_EXPERTS
    n_slots = n_blocks * bm
    flat_tok = jnp.arange(n_assign, dtype=jnp.int32) // TOP_K
    slot_tok = jnp.zeros((n_slots,), jnp.int32).at[dest].set(flat_tok)
    block_e = jnp.minimum(jnp.searchsorted(pad_end, jnp.arange(n_blocks, dtype=jnp.int32) * bm, side='right'),
                          N_EXPERTS - 1).astype(jnp.int32)
    n_used = (pad_end[-1:] // bm).astype(jnp.int32)
    y_sorted = _moe_swiglu(h, slot_tok, w_gate.astype(BF16), w_up.astype(BF16), w_down.astype(BF16),
                           block_e, n_used, bm, tf)
    return _combine(x2, route, y_sorted, dest, _tile(n, 256))


def kernel(x, mix_norm, w_in, sb_q_norm, sb_k_norm, gdn_conv, gdn_a_log, gdn_dt_bias, gdn_out_norm, w_branch,
           w_out, ffn_norm, w_ffn_gate, w_ffn_up, w_ffn_down, w_router, w_exp_gate, w_exp_up, w_exp_down):
    b, t, d = x.shape
    depth = mix_norm.shape[0]
    d_ff = w_ffn_gate.shape[2]
    x2 = x.reshape(b * t, d)
    bm = _tile(b * t, 512)
    tf = _tile(d_ff, 512)
    for layer in range(depth):
        x2 = _mixer(x2, b, t, mix_norm[layer], w_in[layer], sb_q_norm[layer], sb_k_norm[layer], gdn_conv[layer],
                    gdn_a_log[layer], gdn_dt_bias[layer], gdn_out_norm[layer], w_branch[layer], w_out[layer])
        i = layer // 2
        if layer % 2 == 0:
            x2 = _dense_ffn(x2, ffn_norm[layer], w_ffn_gate[i], w_ffn_up[i], w_ffn_down[i], bm, tf)
        else:
            x2 = _moe_ffn(x2, ffn_norm[layer], w_router[i], w_exp_gate[i], w_exp_up[i], w_exp_down[i], bm, tf)
    return x2.reshape(b, t, d)
```

```python
import functools

import numpy as np
import jax
import jax.numpy as jnp
from jax import lax
from jax.experimental import pallas as pl
from jax.experimental.pallas import tpu as pltpu

F32 = jnp.float32
BF16 = jnp.bfloat16

HEAD_DIM = 128
HEADS = 8
WIDTH = HEADS * HEAD_DIM
RET_CHUNK = 128
RET_HEAD_GROUP = 4
SB_TILE = 256
SB_HEADS = 2
SB_DEAD_CARRY = -104.0
GDN_CHUNK = 64
GDN_BLOCK = 256
GDN_HEAD_GROUP = 4
CONV_WIDTH = 4
ROPE_BASE = 10000.0
N_EXPERTS = 8
TOP_K = 2
EPS = 1e-6
LANES = 128
SUBLANES = 8
VMEM_LIMIT = 56 * 1024 * 1024

COL_RQ, COL_RK, COL_RV, COL_RG = 0, WIDTH, 2 * WIDTH, 3 * WIDTH
COL_SQ, COL_SK, COL_SV = 4 * WIDTH, 5 * WIDTH, 6 * WIDTH
COL_GQ, COL_GK, COL_GV, COL_GZ = 7 * WIDTH, 8 * WIDTH, 9 * WIDTH, 10 * WIDTH
COL_GATES = 11 * WIDTH
SMALL_OFF = 10 * WIDTH

NT = (((1,), (1,)), ((), ()))
TN = (((0,), (0,)), ((), ()))


def _params(*sem):
    return pltpu.CompilerParams(dimension_semantics=sem, vmem_limit_bytes=VMEM_LIMIT)


def _dot(a, b):
    return jnp.dot(a.astype(BF16), b.astype(BF16), preferred_element_type=F32)


def _dot_nt(a, b):
    return lax.dot_general(a.astype(BF16), b.astype(BF16), NT, preferred_element_type=F32)


def _dot_tn(a, b):
    return lax.dot_general(a.astype(BF16), b.astype(BF16), TN, preferred_element_type=F32)


def _split3(a):
    hi = a.astype(BF16)
    r1 = a - hi.astype(F32)
    mid = r1.astype(BF16)
    lo = (r1 - mid.astype(F32)).astype(BF16)
    return hi, mid, lo


def _dot_exact_rhs(a, b01):
    return sum(jnp.dot(t, b01, preferred_element_type=F32) for t in _split3(a))


def _dot_exact_lhs(a01, b):
    return sum(jnp.dot(a01, t, preferred_element_type=F32) for t in _split3(b))


def _silu(x):
    return x * jax.nn.sigmoid(x)


def _rms(x):
    return x * lax.rsqrt(jnp.mean(x * x, axis=-1, keepdims=True) + EPS)


def _norm_small_kernel(x_ref, g_ref, ws_ref, wst_ref, h_ref, zs_ref, zst_ref):
    hb = (_rms(x_ref[...]) * g_ref[...]).astype(BF16)
    h_ref[...] = hb
    zs_ref[...] = jnp.dot(hb, ws_ref[...], preferred_element_type=F32)
    zst_ref[...] = lax.dot_general(wst_ref[...], hb, NT, preferred_element_type=F32)


def _norm_small(x2, gain, w_small, w_small_t, tm):
    n, d = x2.shape
    return pl.pallas_call(
        _norm_small_kernel,
        grid=(n // tm,),
        in_specs=[pl.BlockSpec((tm, d), lambda i: (i, 0)),
                  pl.BlockSpec((1, d), lambda i: (0, 0)),
                  pl.BlockSpec((d, LANES), lambda i: (0, 0)),
                  pl.BlockSpec((2 * HEADS, d), lambda i: (0, 0))],
        out_specs=[pl.BlockSpec((tm, d), lambda i: (i, 0)),
                   pl.BlockSpec((tm, LANES), lambda i: (i, 0)),
                   pl.BlockSpec((2 * HEADS, tm), lambda i: (0, i))],
        out_shape=[jax.ShapeDtypeStruct((n, d), BF16),
                   jax.ShapeDtypeStruct((n, LANES), F32),
                   jax.ShapeDtypeStruct((2 * HEADS, n), F32)],
        compiler_params=_params("parallel"),
        name="norm_small",
    )(x2, gain, w_small, w_small_t)


def _mm_kernel(a_ref, w_ref, o_ref):
    o_ref[...] = jnp.dot(a_ref[...], w_ref[...], preferred_element_type=F32).astype(o_ref.dtype)


def _matmul(a, w, tm, tn, out_dtype):
    m, k = a.shape
    n = w.shape[1]
    return pl.pallas_call(
        _mm_kernel,
        grid=(m // tm, n // tn),
        in_specs=[pl.BlockSpec((tm, k), lambda i, j: (i, 0)),
                  pl.BlockSpec((k, tn), lambda i, j: (0, j))],
        out_specs=pl.BlockSpec((tm, tn), lambda i, j: (i, j)),
        out_shape=jax.ShapeDtypeStruct((m, n), out_dtype),
        compiler_params=_params("parallel", "arbitrary"),
        name="matmul",
    )(a, w)


def _mm_res_kernel(a_ref, w_ref, r_ref, o_ref):
    o_ref[...] = r_ref[...] + jnp.dot(a_ref[...], w_ref[...], preferred_element_type=F32)


def _matmul_residual(a, w, res, tm, tn):
    m, k = a.shape
    n = w.shape[1]
    return pl.pallas_call(
        _mm_res_kernel,
        grid=(m // tm, n // tn),
        in_specs=[pl.BlockSpec((tm, k), lambda i, j: (i, 0)),
                  pl.BlockSpec((k, tn), lambda i, j: (0, j)),
                  pl.BlockSpec((tm, tn), lambda i, j: (i, j))],
        out_specs=pl.BlockSpec((tm, tn), lambda i, j: (i, j)),
        out_shape=jax.ShapeDtypeStruct((m, n), F32),
        compiler_params=_params("parallel", "arbitrary"),
        name="matmul_residual",
    )(a, w, res)


def _ret_kernel(q_ref, k_ref, v_ref, g_ref, cos_ref, sin_ref, inner_ref, qh_ref, kt_ref, cd_ref,
                o_ref, s_ref, *, n_chunks):
    @pl.when(pl.program_id(1) == 0)
    def _():
        s_ref[...] = jnp.zeros_like(s_ref)

    scale = HEAD_DIM ** -0.5
    c = RET_CHUNK
    each = lambda fn, *lists: [fn(*args) for args in zip(*lists)]
    rot = lambda x, cos, sin: x * cos + pltpu.roll(x, HEAD_DIM // 2, 1) * sin
    for ci in range(n_chunks):
        rows = slice(ci * c, (ci + 1) * c)
        cos, sin = cos_ref[rows, :], sin_ref[rows, :]
        for h0 in range(0, HEADS, RET_HEAD_GROUP):
            hs = list(range(h0, h0 + RET_HEAD_GROUP))
            sls = [slice(hh * HEAD_DIM, (hh + 1) * HEAD_DIM) for hh in hs]
            qr = [rot(q_ref[rows, sl], cos, sin) for sl in sls]
            kr = [rot(k_ref[rows, sl], cos, sin) * scale for sl in sls]
            vb = [v_ref[rows, sl].astype(BF16) for sl in sls]
            state = [s_ref[hh] for hh in hs]
            scores = each(lambda a, b, hh: _dot_nt(a, b) * inner_ref[hh], qr, kr, hs)
            o_inner = each(_dot, scores, vb)
            o_cross = each(lambda a, s, hh: _dot(a * qh_ref[hh], s), qr, state, hs)
            kv = each(lambda a, b, hh: _dot_tn(a * kt_ref[hh], b), kr, vb, hs)
            for hh, s, d in zip(hs, state, kv):
                s_ref[hh] = s * cd_ref[hh][0:1, :] + d
            for sl, a, b in zip(sls, o_inner, o_cross):
                o_ref[rows, sl] = (_rms(a + b) * _silu(g_ref[rows, sl])).astype(BF16)


def _retention_tables(t):
    h = np.arange(HEADS, dtype=np.float64)
    log_gamma = np.log1p(-np.exp2(-5.0 - h))
    pos = np.arange(RET_CHUNK, dtype=np.float64)
    rel = pos[:, None] - pos[None, :]
    inner = np.where(rel >= 0, np.exp(np.maximum(rel, 0.0)[None] * log_gamma[:, None, None]), 0.0)
    k_tail = np.exp((RET_CHUNK - 1.0 - pos)[None, :] * log_gamma[:, None])
    q_head = np.exp((pos + 1.0)[None, :] * log_gamma[:, None])
    chunk_decay = np.exp(RET_CHUNK * log_gamma)
    half = HEAD_DIM // 2
    inv_freq = ROPE_BASE ** (-np.arange(half, dtype=np.float64) / half)
    ang = np.arange(t, dtype=np.float64)[:, None] * inv_freq[None, :]
    cos = np.concatenate([np.cos(ang), np.cos(ang)], axis=1)
    sin = np.concatenate([-np.sin(ang), np.sin(ang)], axis=1)
    bc = lambda a: np.broadcast_to(a[:, :, None], (HEADS, RET_CHUNK, HEAD_DIM))
    f = lambda a: jnp.asarray(np.ascontiguousarray(a), F32)
    return (f(cos), f(sin), f(inner), f(bc(q_head)), f(bc(k_tail)),
            f(np.broadcast_to(chunk_decay[:, None, None], (HEADS, SUBLANES, HEAD_DIM))))


def _retention(z, b, t, tb):
    n = b * t
    nt = t // tb
    cos, sin, inner, q_head, k_tail, chunk_decay = _retention_tables(t)
    col = lambda off: pl.BlockSpec((tb, WIDTH), lambda bi, i, off=off: (bi * nt + i, off // WIDTH))
    pos = pl.BlockSpec((tb, HEAD_DIM), lambda bi, i: (i, 0))
    head_tab = lambda r: pl.BlockSpec((HEADS, r, HEAD_DIM), lambda bi, i: (0, 0, 0))
    return pl.pallas_call(
        functools.partial(_ret_kernel, n_chunks=tb // RET_CHUNK),
        grid=(b, nt),
        in_specs=[col(COL_RQ), col(COL_RK), col(COL_RV), col(COL_RG), pos, pos,
                  head_tab(RET_CHUNK), head_tab(RET_CHUNK), head_tab(RET_CHUNK), head_tab(SUBLANES)],
        out_specs=pl.BlockSpec((tb, WIDTH), lambda bi, i: (bi * nt + i, 0)),
        out_shape=jax.ShapeDtypeStruct((n, WIDTH), BF16),
        scratch_shapes=[pltpu.VMEM((HEADS, HEAD_DIM, HEAD_DIM), F32)],
        compiler_params=_params("parallel", "arbitrary"),
        name="retention",
    )(z, z, z, z, cos, sin, inner, q_head, k_tail, chunk_decay)


def _split2(a):
    hi = a.astype(BF16)
    return hi, (a - hi.astype(F32)).astype(BF16)


def _sb_kernel(q_ref, k_ref, v_ref, qn_ref, kn_ref, u_ref, o_ref, kb_ref, vb_ref, acc_ref, carry_ref, *, prep_rows):
    i = pl.program_id(2)
    t = k_ref.shape[0]
    bq = q_ref.shape[0]
    heads = [slice(h * HEAD_DIM, (h + 1) * HEAD_DIM) for h in range(SB_HEADS)]

    @pl.when(i == 0)
    def _():
        def prep(c, carry):
            rows = pl.ds(pl.multiple_of(c * prep_rows, prep_rows), prep_rows)
            for sl in heads:
                kb_ref[rows, sl] = (_rms(k_ref[rows, sl]) * kn_ref[...]).astype(BF16)
            vb_ref[rows, :] = v_ref[rows, :].astype(BF16)
            return carry
        lax.fori_loop(0, t // prep_rows, prep, 0)

    qb = [(_rms(q_ref[:, sl]) * qn_ref[...] * HEAD_DIM ** -0.5).astype(BF16) for sl in heads]
    u = u_ref[...]
    each = lambda fn, *lists: [fn(*args) for args in zip(*lists)]
    d32 = lambda a, b: jnp.dot(a, b, preferred_element_type=F32)
    streams = [(h, far) for h in range(SB_HEADS) for far in (0, 1)]

    def tile_pair(j, diagonal):
        has_far = j >= 1
        starts = [pl.multiple_of(j * bq, bq), pl.multiple_of(jnp.maximum(j - 1, 0) * bq, bq)]
        z = [lax.dot_general(qb[h], kb_ref[pl.ds(starts[far], bq), heads[h]], NT, preferred_element_type=F32)
             for h, far in streams]
        log_not = each(lambda x: -(jnp.maximum(x, 0.0) + jnp.log(1.0 + jnp.exp(-jnp.abs(x)))), z)
        if diagonal:
            causal = lax.broadcasted_iota(jnp.int32, (bq, bq), 1) < lax.broadcasted_iota(jnp.int32, (bq, bq), 0)
        for n, (h, far) in enumerate(streams):
            if far:
                log_not[n] = jnp.where(has_far, log_not[n], 0.0)
            elif diagonal:
                log_not[n] = jnp.where(causal, log_not[n], 0.0)
        split = each(_split2, log_not)
        later = [d32(hi, u) + d32(lo, u) for hi, lo in split]
        total = each(lambda x: jnp.sum(x, axis=1, keepdims=True), log_not)
        logw = each(lambda a, b, c: a + b + c, log_not, z, later)
        tops = []
        for h in range(SB_HEADS):
            near, far = 2 * h, 2 * h + 1
            if diagonal:
                w_near = jnp.where(causal, jnp.exp(logw[near]), 0.0)
                before_far = total[near]
            else:
                old = carry_ref[h]
                w_near = jnp.exp(logw[near] + old)
                before_far = old + total[near]
            w_far = jnp.where(has_far, jnp.exp(logw[far] + before_far), 0.0)
            pv = (d32(w_near.astype(BF16), vb_ref[pl.ds(starts[0], bq), heads[h]])
                  + d32(w_far.astype(BF16), vb_ref[pl.ds(starts[1], bq), heads[h]]))
            acc_ref[:, heads[h]] = pv if diagonal else acc_ref[:, heads[h]] + pv
            carry = before_far + total[far]
            carry_ref[h] = carry
            tops.append(jnp.max(carry))
        return functools.reduce(jnp.maximum, tops)

    def live(state):
        return (state[0] >= 0) & (state[1] > SB_DEAD_CARRY)

    def step(state):
        return state[0] - 2, tile_pair(state[0], False)

    lax.while_loop(live, step, (i - 2, tile_pair(i, True)))
    o_ref[...] = acc_ref[...].astype(BF16)


def _stick_breaking(z, qn, kn, b, t):
    n = b * t
    bq = _tile(t, SB_TILE)
    nt = t // bq
    w = SB_HEADS * HEAD_DIM
    j = np.arange(bq)
    u = jnp.asarray(j[:, None] > j[None, :], BF16)
    seq = lambda off: pl.BlockSpec((t, w), lambda bi, g, i, off=off: (bi, off // w + g))
    return pl.pallas_call(
        functools.partial(_sb_kernel, prep_rows=bq),
        grid=(b, HEADS // SB_HEADS, nt),
        in_specs=[pl.BlockSpec((bq, w), lambda bi, g, i: (bi * nt + i, COL_SQ // w + g)),
                  seq(COL_SK), seq(COL_SV),
                  pl.BlockSpec((1, HEAD_DIM), lambda bi, g, i: (0, 0)),
                  pl.BlockSpec((1, HEAD_DIM), lambda bi, g, i: (0, 0)),
                  pl.BlockSpec((bq, bq), lambda bi, g, i: (0, 0))],
        out_specs=pl.BlockSpec((bq, w), lambda bi, g, i: (bi * nt + i, g)),
        out_shape=jax.ShapeDtypeStruct((n, WIDTH), BF16),
        scratch_shapes=[pltpu.VMEM((t, w), BF16), pltpu.VMEM((t, w), BF16),
                        pltpu.VMEM((bq, w), F32), pltpu.VMEM((SB_HEADS, bq, 1), F32)],
        compiler_params=_params("parallel", "parallel", "arbitrary"),
        name="stick_breaking",
    )(z, z, z, qn, kn, u)


def _gdn_kernel(xq_ref, xk_ref, xv_ref, gz_ref, cq_ref, ck_ref, cv_ref, zs_ref, gat_ref, alog_ref, dt_ref,
                alog_t_ref, dt_t_ref, cum_ref, cum_t_ref, gn_ref, o_ref, bq_ref, bk_ref, bv_ref, s_ref):
    tb = GDN_BLOCK
    c = GDN_CHUNK
    halo = SUBLANES

    @pl.when(pl.program_id(1) == 0)
    def _():
        s_ref[...] = jnp.zeros_like(s_ref)
        for buf in (bq_ref, bk_ref, bv_ref):
            buf[0:halo, :] = jnp.zeros((halo, WIDTH), F32)

    def conv(x_ref, cw_ref, buf):
        buf[halo:halo + tb, :] = x_ref[...]
        y = cw_ref[CONV_WIDTH - 1:CONV_WIDTH, :] * buf[halo:halo + tb, :]
        for tap in range(CONV_WIDTH - 1):
            back = CONV_WIDTH - 1 - tap
            y = y + cw_ref[tap:tap + 1, :] * buf[halo - back:halo - back + tb, :]
        buf[0:halo, :] = buf[tb:tb + halo, :]
        buf[halo:halo + tb, :] = _silu(y)

    conv(xq_ref, cq_ref, bq_ref)
    conv(xk_ref, ck_ref, bk_ref)
    conv(xv_ref, cv_ref, bv_ref)

    zs = zs_ref[...]
    beta_lanes = jax.nn.sigmoid(zs)
    g_lanes = -jnp.exp(alog_ref[...]) * jax.nn.softplus(zs + dt_ref[...])
    gc_lanes = _dot_exact_lhs(cum_ref[...], g_lanes)
    g_t = -jnp.exp(alog_t_ref[...]) * jax.nn.softplus(gat_ref[...] + dt_t_ref[...])
    gc_t = _dot_exact_rhs(g_t, cum_t_ref[...])

    ri = lax.broadcasted_iota(jnp.int32, (tb, tb), 0)
    ci = lax.broadcasted_iota(jnp.int32, (tb, tb), 1)
    same = (ri // c) == (ci // c)
    lower = same & (ri >= ci)
    strict = same & (ri > ci)
    scale = HEAD_DIM ** -0.5
    gn = gn_ref[...]

    def each(fn, *lists):
        return [fn(*args) for args in zip(*lists)]

    def wide(m):
        return functools.reduce(lambda x, y: x + y, [m[cc * c:(cc + 1) * c, :] for cc in range(tb // c)])

    def block_diag(m):
        return jnp.where(same, jnp.concatenate([m] * (tb // c), axis=0), 0.0)

    eye_wide = (lax.broadcasted_iota(jnp.int32, (c, tb), 0) == lax.broadcasted_iota(jnp.int32, (c, tb), 1) % c).astype(F32)

    for h0 in range(0, HEADS, GDN_HEAD_GROUP):
        hs = list(range(h0, h0 + GDN_HEAD_GROUP))
        sls = [slice(hh * HEAD_DIM, (hh + 1) * HEAD_DIM) for hh in hs]
        q = [bq_ref[halo:halo + tb, sl] for sl in sls]
        k = [bk_ref[halo:halo + tb, sl] for sl in sls]
        v = [bv_ref[halo:halo + tb, sl] for sl in sls]
        q = each(lambda x: x * lax.rsqrt(jnp.sum(x * x, axis=-1, keepdims=True) + EPS) * scale, q)
        k = each(lambda x: x * lax.rsqrt(jnp.sum(x * x, axis=-1, keepdims=True) + EPS), k)
        beta = [beta_lanes[:, hh:hh + 1] for hh in hs]
        gc = [gc_lanes[:, HEADS + hh:HEADS + hh + 1] for hh in hs]
        gc_rows = [gc_t[hh:hh + 1, :] for hh in hs]
        decay = each(lambda gi, gj: jnp.where(lower, jnp.exp(jnp.where(lower, gi - gj, 0.0)), 0.0), gc, gc_rows)
        k_beta = each(lambda x, y: x * y, k, beta)
        kk = each(_dot_nt, k_beta, k)
        a = each(lambda x, d: jnp.where(strict, x * d, 0.0), kk, decay)
        x = each(lambda y: eye_wide - wide(y), a)
        p = each(lambda y: _dot(wide(y), y), a)
        for step in range(5):
            p_bd = each(block_diag, p)
            x = each(lambda y, d: y + _dot(y, d), x, p_bd)
            if step < 4:
                p = each(_dot, p, p_bd)
        t_inv = each(block_diag, x)
        e_gc = each(jnp.exp, gc)
        rhs = each(lambda vv, bb, kb, eg: jnp.concatenate([vv * bb, kb * eg], axis=1), v, beta, k_beta, e_gc)
        uw = each(_dot, t_inv, rhs)
        qk = each(lambda y, d: jnp.where(lower, y * d, 0.0), each(_dot_nt, q, k), decay)
        q_dec = each(lambda y, eg: y * eg, q, e_gc)
        for cc in range(tb // c):
            rows = slice(cc * c, (cc + 1) * c)
            g_last = [g[cc * c + c - 1:cc * c + c, :] for g in gc]
            k_dec = each(lambda kx, gl, g: kx[rows] * jnp.exp(gl - g[rows]), k, g_last, gc)
            state = [s_ref[hh] for hh in hs]
            ws = each(lambda y, s: _dot(y[rows, HEAD_DIM:], s), uw, state)
            v_new = each(lambda y, d: y[rows, :HEAD_DIM] - d, uw, ws)
            o_cross = each(lambda y, s: _dot(y[rows], s), q_dec, state)
            o_intra = each(lambda y, vn: _dot(y[rows, cc * c:(cc + 1) * c], vn), qk, v_new)
            kv = each(_dot_tn, k_dec, v_new)
            for hh, s, gl, d in zip(hs, state, g_last, kv):
                s_ref[hh] = s * jnp.exp(gl) + d
            for sl, oc, oi in zip(sls, o_cross, o_intra):
                o = oc + oi
                o_ref[rows, sl] = (_rms(o) * gn * _silu(gz_ref[rows, sl])).astype(BF16)


def _gated_deltanet(z, zs, zst, conv_w, a_log, dt_bias, out_norm, b, t):
    n = b * t
    tb, c = GDN_BLOCK, GDN_CHUNK
    nt = t // tb
    idx = np.arange(tb)
    same = (idx[:, None] // c) == (idx[None, :] // c)
    cum = (same & (idx[:, None] >= idx[None, :])).astype(np.float32)
    cum, cum_t = jnp.asarray(cum, BF16), jnp.asarray(cum.T, BF16)
    lanes = lambda a1: jnp.pad(a1, (HEADS, LANES - 2 * HEADS))[None, :]
    rows = lambda a1: jnp.broadcast_to(a1[:, None], (HEADS, tb))

    def col(off):
        return pl.BlockSpec((tb, WIDTH), lambda bi, i, off=off: (bi * nt + i, off // WIDTH))

    def ccol(blk):
        return pl.BlockSpec((CONV_WIDTH, WIDTH), lambda bi, i, blk=blk: (0, blk))

    const = lambda shape: pl.BlockSpec(shape, lambda bi, i: (0,) * len(shape))
    return pl.pallas_call(
        _gdn_kernel,
        grid=(b, nt),
        in_specs=[col(COL_GQ), col(COL_GK), col(COL_GV), col(COL_GZ), ccol(0), ccol(1), ccol(2),
                  pl.BlockSpec((tb, LANES), lambda bi, i: (bi * nt + i, 0)),
                  pl.BlockSpec((HEADS, tb), lambda bi, i: (1, bi * nt + i)),
                  const((1, LANES)), const((1, LANES)), const((HEADS, tb)), const((HEADS, tb)),
                  const((tb, tb)), const((tb, tb)), const((1, HEAD_DIM))],
        out_specs=pl.BlockSpec((tb, WIDTH), lambda bi, i: (bi * nt + i, 0)),
        out_shape=jax.ShapeDtypeStruct((n, WIDTH), BF16),
        scratch_shapes=[pltpu.VMEM((tb + SUBLANES, WIDTH), F32)] * 3 + [pltpu.VMEM((HEADS, HEAD_DIM, HEAD_DIM), F32)],
        compiler_params=_params("parallel", "arbitrary"),
        name="gated_deltanet",
    )(z, z, z, z, conv_w, conv_w, conv_w, zs, zst, lanes(a_log), lanes(dt_bias), rows(a_log), rows(dt_bias),
      cum, cum_t, out_norm)


def _merge_kernel(or_ref, os_ref, og_ref, wb_ref, gr_ref, gs_ref, gg_ref, o_ref):
    acc = jax.nn.sigmoid(gr_ref[...]) * jnp.dot(or_ref[...], wb_ref[0], preferred_element_type=F32)
    acc = acc + jax.nn.sigmoid(gs_ref[...]) * jnp.dot(os_ref[...], wb_ref[1], preferred_element_type=F32)
    acc = acc + jax.nn.sigmoid(gg_ref[...]) * jnp.dot(og_ref[...], wb_ref[2], preferred_element_type=F32)
    o_ref[...] = acc.astype(BF16)


def _merge(o_r, o_s, o_g, w_branch, z, d, tm, tn):
    n = o_r.shape[0]
    assert COL_GATES % tn == 0 and d % tn == 0
    branch = pl.BlockSpec((tm, WIDTH), lambda j, i: (i, 0))
    gate = lambda bidx: pl.BlockSpec((tm, tn), lambda j, i, bidx=bidx: (i, (COL_GATES + bidx * d) // tn + j))
    return pl.pallas_call(
        _merge_kernel,
        grid=(d // tn, n // tm),
        in_specs=[branch, branch, branch,
                  pl.BlockSpec((3, WIDTH, tn), lambda j, i: (0, 0, j)),
                  gate(0), gate(1), gate(2)],
        out_specs=pl.BlockSpec((tm, tn), lambda j, i: (i, j)),
        out_shape=jax.ShapeDtypeStruct((n, d), BF16),
        compiler_params=_params("parallel", "parallel"),
        name="branch_merge",
    )(o_r, o_s, o_g, w_branch, z, z, z)


def _norm_kernel(x_ref, g_ref, h_ref):
    h_ref[...] = (_rms(x_ref[...]) * g_ref[...]).astype(h_ref.dtype)


def _norm(x2, gain, tm):
    n, d = x2.shape
    return pl.pallas_call(
        _norm_kernel,
        grid=(n // tm,),
        in_specs=[pl.BlockSpec((tm, d), lambda i: (i, 0)), pl.BlockSpec((1, d), lambda i: (0, 0))],
        out_specs=pl.BlockSpec((tm, d), lambda i: (i, 0)),
        out_shape=jax.ShapeDtypeStruct((n, d), BF16),
        compiler_params=_params("parallel"),
        name="norm",
    )(x2, gain)


def _norm_router_kernel(x_ref, g_ref, wr_ref, h_ref, rt_ref):
    h = _rms(x_ref[...]) * g_ref[...]
    h_ref[...] = h
    (h_hi, h_lo), (w_hi, w_lo) = _split2(h), _split2(wr_ref[...])
    d32 = lambda a, b: jnp.dot(a, b, preferred_element_type=F32)
    logits = d32(h_hi, w_hi) + (d32(h_hi, w_lo) + d32(h_lo, w_hi))
    lane = lax.broadcasted_iota(jnp.int32, logits.shape, 1)
    neg = jnp.float32(-jnp.inf)
    lg = jnp.where(lane < N_EXPERTS, logits, neg)
    m1 = jnp.max(lg, axis=-1, keepdims=True)
    i1 = jnp.min(jnp.where(lg == m1, lane, LANES), axis=-1, keepdims=True)
    lg2 = jnp.where(lane == i1, neg, lg)
    m2 = jnp.max(lg2, axis=-1, keepdims=True)
    i2 = jnp.min(jnp.where(lg2 == m2, lane, LANES), axis=-1, keepdims=True)
    e = jnp.exp(m2 - m1)
    g1 = 1.0 / (1.0 + e)
    g2 = e / (1.0 + e)
    rt_ref[...] = jnp.where(lane == 0, i1.astype(F32),
                            jnp.where(lane == 1, i2.astype(F32),
                                      jnp.where(lane == 2, g1, jnp.where(lane == 3, g2, 0.0))))


def _norm_router(x2, gain, w_router_pad, tm):
    n, d = x2.shape
    return pl.pallas_call(
        _norm_router_kernel,
        grid=(n // tm,),
        in_specs=[pl.BlockSpec((tm, d), lambda i: (i, 0)), pl.BlockSpec((1, d), lambda i: (0, 0)),
                  pl.BlockSpec((d, LANES), lambda i: (0, 0))],
        out_specs=[pl.BlockSpec((tm, d), lambda i: (i, 0)), pl.BlockSpec((tm, LANES), lambda i: (i, 0))],
        out_shape=[jax.ShapeDtypeStruct((n, d), F32), jax.ShapeDtypeStruct((n, LANES), F32)],
        compiler_params=_params("parallel"),
        name="norm_router",
    )(x2, gain, w_router_pad)


def _dense_swiglu_kernel(x_ref, wg_ref, wu_ref, wd_ref, r_ref, o_ref):
    @pl.when(pl.program_id(1) == 0)
    def _():
        o_ref[...] = r_ref[...]

    xb = x_ref[...]
    gate = jnp.dot(xb, wg_ref[...], preferred_element_type=F32)
    up = jnp.dot(xb, wu_ref[...], preferred_element_type=F32)
    act = (_silu(gate) * up).astype(BF16)
    o_ref[...] += jnp.dot(act, wd_ref[...], preferred_element_type=F32)


def _dense_swiglu(h, w_gate, w_up, w_down, residual, bm, tf):
    n, d = h.shape
    d_ff = w_gate.shape[1]
    rows = pl.BlockSpec((bm, d), lambda i, f: (i, 0))
    return pl.pallas_call(
        _dense_swiglu_kernel,
        grid=(n // bm, d_ff // tf),
        in_specs=[rows, pl.BlockSpec((d, tf), lambda i, f: (0, f)), pl.BlockSpec((d, tf), lambda i, f: (0, f)),
                  pl.BlockSpec((tf, d), lambda i, f: (f, 0)), rows],
        out_specs=rows,
        out_shape=jax.ShapeDtypeStruct((n, d), F32),
        compiler_params=_params("parallel", "arbitrary"),
        name="dense_swiglu",
    )(h, w_gate, w_up, w_down, residual)


def _row_copy(src_hbm, dst_ref, src_row, dst_row, sem):
    return pltpu.make_async_copy(src_hbm.at[pl.ds(src_row, 1), :], dst_ref.at[pl.ds(dst_row, 1), :], sem)


def _moe_swiglu_kernel(be_ref, nu_ref, tok_ref, h_hbm, wg_ref, wu_ref, wd_ref, o_ref, xg_ref, xb_ref, sem_ref,
                       *, bm, nf):
    blk, f = pl.program_id(0), pl.program_id(1)
    n_used = nu_ref[0]
    slot = blk % 2
    rows_per_step = -(-bm // nf)
    fetched = rows_per_step * nf

    def fetch(block, row, to_slot):
        return _row_copy(h_hbm, xg_ref.at[to_slot], tok_ref[block * bm + row], row, sem_ref.at[to_slot])

    def wait_fetched(of_slot):
        def wait(r, carry):
            _row_copy(h_hbm, xg_ref.at[of_slot], 0, r, sem_ref.at[of_slot]).wait()
            return carry
        lax.fori_loop(0, fetched, wait, 0, unroll=8)

    @pl.when((blk == 0) & (f == 0))
    def _():
        def start(r, carry):
            fetch(0, r, 0).start()
            return carry
        lax.fori_loop(0, fetched, start, 0)

    @pl.when(blk < n_used)
    def _():
        @pl.when(f == 0)
        def _():
            wait_fetched(slot)
            xb_ref[...] = xg_ref[slot, 0:bm, :].astype(BF16)
            o_ref[...] = jnp.zeros_like(o_ref)

        for r in range(rows_per_step):
            fetch(blk + 1, f * rows_per_step + r, 1 - slot).start()

        xb = xb_ref[...]
        gate = jnp.dot(xb, wg_ref[0], preferred_element_type=F32)
        up = jnp.dot(xb, wu_ref[0], preferred_element_type=F32)
        act = (_silu(gate) * up).astype(BF16)
        o_ref[...] += jnp.dot(act, wd_ref[0], preferred_element_type=F32)

    @pl.when((blk >= n_used) & (f == 0))
    def _():
        @pl.when(blk == n_used)
        def _():
            wait_fetched(slot)
        o_ref[...] = jnp.zeros_like(o_ref)


def _moe_swiglu(h, slot_tok, w_gate, w_up, w_down, block_expert, n_used, bm, tf):
    d = h.shape[1]
    d_ff = w_gate.shape[2]
    nb, nf = slot_tok.shape[0] // bm, d_ff // tf
    buffer_rows = -(-(-(-bm // nf) * nf) // SUBLANES) * SUBLANES

    def f_idx(blk, f, nu):
        return jnp.where(blk < nu[0], f, nf - 1)

    def w_in_map(blk, f, be, nu, tok):
        return (be[jnp.minimum(blk, nu[0] - 1)], 0, f_idx(blk, f, nu))

    def w_out_map(blk, f, be, nu, tok):
        return (be[jnp.minimum(blk, nu[0] - 1)], f_idx(blk, f, nu), 0)

    return pl.pallas_call(
        functools.partial(_moe_swiglu_kernel, bm=bm, nf=nf),
        grid_spec=pltpu.PrefetchScalarGridSpec(
            num_scalar_prefetch=3, grid=(nb, nf),
            in_specs=[pl.BlockSpec(memory_space=pl.ANY),
                      pl.BlockSpec((1, d, tf), w_in_map), pl.BlockSpec((1, d, tf), w_in_map),
                      pl.BlockSpec((1, tf, d), w_out_map)],
            out_specs=pl.BlockSpec((bm, d), lambda blk, f, be, nu, tok: (blk, 0)),
            scratch_shapes=[pltpu.VMEM((2, buffer_rows, d), F32), pltpu.VMEM((bm, d), BF16),
                            pltpu.SemaphoreType.DMA((2,))]),
        out_shape=jax.ShapeDtypeStruct((nb * bm, d), F32),
        compiler_params=_params("arbitrary", "arbitrary"),
        name="moe_swiglu",
    )(block_expert, n_used, slot_tok, h, w_gate, w_up, w_down)


def _combine_kernel(pos_ref, x_ref, rt_ref, y_hbm, o_ref, b0_ref, b1_ref, sem, *, tm):
    base = pl.program_id(0) * tm

    def start(r, carry):
        _row_copy(y_hbm, b0_ref, pos_ref[2 * (base + r)], r, sem).start()
        _row_copy(y_hbm, b1_ref, pos_ref[2 * (base + r) + 1], r, sem).start()
        return carry

    def wait(r, carry):
        _row_copy(y_hbm, b0_ref, 0, r, sem).wait()
        _row_copy(y_hbm, b1_ref, 0, r, sem).wait()
        return carry

    lax.fori_loop(0, tm, start, 0, unroll=8)
    lax.fori_loop(0, tm, wait, 0, unroll=8)
    rt = rt_ref[...]
    o_ref[...] = x_ref[...] + (b0_ref[...] * rt[:, 2:3] + b1_ref[...] * rt[:, 3:4])


def _combine(x2, route, y_sorted, pos, tm):
    n, d = x2.shape
    return pl.pallas_call(
        functools.partial(_combine_kernel, tm=tm),
        grid_spec=pltpu.PrefetchScalarGridSpec(
            num_scalar_prefetch=1, grid=(n // tm,),
            in_specs=[pl.BlockSpec((tm, d), lambda i, p: (i, 0)),
                      pl.BlockSpec((tm, LANES), lambda i, p: (i, 0)),
                      pl.BlockSpec(memory_space=pl.ANY)],
            out_specs=pl.BlockSpec((tm, d), lambda i, p: (i, 0)),
            scratch_shapes=[pltpu.VMEM((tm, d), F32), pltpu.VMEM((tm, d), F32), pltpu.SemaphoreType.DMA(())]),
        out_shape=jax.ShapeDtypeStruct((n, d), F32),
        compiler_params=_params("arbitrary"),
        name="combine",
    )(pos, x2, route, y_sorted)


def _tile(n, want):
    for cand in range(min(n, want), 0, -LANES):
        if n % cand == 0:
            return cand
    raise ValueError((n, want))


def _mixer(x2, b, t, norm_w, w_in, sb_qn, sb_kn, conv_w, a_log, dt_bias, gdn_on, w_branch, w_out):
    n, d = x2.shape
    w_main = jnp.concatenate([w_in[:, :SMALL_OFF], w_in[:, SMALL_OFF + 2 * HEADS:]], axis=1).astype(BF16)
    w_small = w_in[:, SMALL_OFF:SMALL_OFF + 2 * HEADS]
    w_small_pad = jnp.pad(w_small, ((0, 0), (0, LANES - 2 * HEADS))).astype(BF16)
    hidden, zs, zst = _norm_small(x2, norm_w[None, :], w_small_pad, w_small.T.astype(BF16), _tile(n, 512))
    z = _matmul(hidden, w_main, _tile(n, 2048), _tile(w_main.shape[1], 1024), F32)
    o_r = _retention(z, b, t, _tile(t, 512))
    o_s = _stick_breaking(z, sb_qn[None, :], sb_kn[None, :], b, t)
    o_g = _gated_deltanet(z, zs, zst, conv_w, a_log, dt_bias, gdn_on[None, :], b, t)
    merged = _merge(o_r, o_s, o_g, w_branch.astype(BF16), z, d, _tile(n, 512), _tile(d, 1024))
    return _matmul_residual(merged, w_out.astype(BF16), x2, _tile(n, 1024), _tile(d, 1024))


def _dense_ffn(x2, norm_w, w_gate, w_up, w_down, bm, tf):
    n = x2.shape[0]
    h = _norm(x2, norm_w[None, :], _tile(n, 512))
    return _dense_swiglu(h, w_gate.astype(BF16), w_up.astype(BF16), w_down.astype(BF16), x2, bm, tf)


def _moe_ffn(x2, norm_w, w_router, w_gate, w_up, w_down, bm, tf):
    n, d = x2.shape
    h, route = _norm_router(x2, norm_w[None, :], jnp.pad(w_router, ((0, 0), (0, LANES - N_EXPERTS))),
                            _tile(n, 512))
    flat_e = route[:, :TOP_K].astype(jnp.int32).reshape(-1)
    n_assign = n * TOP_K
    onehot = (flat_e[:, None] == jnp.arange(N_EXPERTS, dtype=jnp.int32)[None, :]).astype(jnp.int32)
    running = jnp.cumsum(onehot, axis=0)
    counts = running[-1]
    rank = jnp.sum((running - onehot) * onehot, axis=1)
    padded = (counts + bm - 1) // bm * bm
    pad_end = jnp.cumsum(padded)
    pad_start = pad_end - padded
    dest = (pad_start[flat_e] + rank).astype(jnp.int32)
    n_blocks = -(-n_assign // bm) + N_EXPERTS
    n_slots = n_blocks * bm
    flat_tok = jnp.arange(n_assign, dtype=jnp.int32) // TOP_K
    slot_tok = jnp.zeros((n_slots,), jnp.int32).at[dest].set(flat_tok)
    block_e = jnp.minimum(jnp.searchsorted(pad_end, jnp.arange(n_blocks, dtype=jnp.int32) * bm, side='right'),
                          N_EXPERTS - 1).astype(jnp.int32)
    n_used = (pad_end[-1:] // bm).astype(jnp.int32)
    y_sorted = _moe_swiglu(h, slot_tok, w_gate.astype(BF16), w_up.astype(BF16), w_down.astype(BF16),
                           block_e, n_used, bm, tf)
    return _combine(x2, route, y_sorted, dest, _tile(n, 256))


def kernel(x, mix_norm, w_in, sb_q_norm, sb_k_norm, gdn_conv, gdn_a_log, gdn_dt_bias, gdn_out_norm, w_branch,
           w_out, ffn_norm, w_ffn_gate, w_ffn_up, w_ffn_down, w_router, w_exp_gate, w_exp_up, w_exp_down):
    b, t, d = x.shape
    depth = mix_norm.shape[0]
    d_ff = w_ffn_gate.shape[2]
    x2 = x.reshape(b * t, d)
    bm = _tile(b * t, 512)
    tf = _tile(d_ff, 512)
    for layer in range(depth):
        x2 = _mixer(x2, b, t, mix_norm[layer], w_in[layer], sb_q_norm[layer], sb_k_norm[layer], gdn_conv[layer],
                    gdn_a_log[layer], gdn_dt_bias[layer], gdn_out_norm[layer], w_branch[layer], w_out[layer])
        i = layer // 2
        if layer % 2 == 0:
            x2 = _dense_ffn(x2, ffn_norm[layer], w_ffn_gate[i], w_ffn_up[i], w_ffn_down[i],
                            _tile(b * t, 2 * bm), _tile(d_ff, tf // 2))
        else:
            x2 = _moe_ffn(x2, ffn_norm[layer], w_router[i], w_exp_gate[i], w_exp_up[i], w_exp_down[i], bm, tf)
    return x2.reshape(b, t, d)
```

```python
import functools
import math

import numpy as np
import jax
import jax.numpy as jnp
from jax import lax
from jax.experimental import pallas as pl
from jax.experimental.pallas import tpu as pltpu

F32 = jnp.float32
BF16 = jnp.bfloat16

HEAD_DIM = 128
HEADS = 8
WIDTH = HEADS * HEAD_DIM
RET_CHUNK = 128
RET_HEAD_GROUP = 4
SB_TILE = 256
SB_HEADS = 2
SB_DEAD_CARRY = -104.0
GDN_CHUNK = 64
GDN_BLOCK = 256
GDN_HEAD_GROUP = 4
CONV_WIDTH = 4
ROPE_BASE = 10000.0
N_EXPERTS = 8
TOP_K = 2
EPS = 1e-6
LANES = 128
SUBLANES = 8
VMEM_LIMIT = 56 * 1024 * 1024

COL_RQ, COL_RK, COL_RV, COL_RG = 0, WIDTH, 2 * WIDTH, 3 * WIDTH
COL_SQ, COL_SK, COL_SV = 4 * WIDTH, 5 * WIDTH, 6 * WIDTH
COL_GQ, COL_GK, COL_GV, COL_GZ = 7 * WIDTH, 8 * WIDTH, 9 * WIDTH, 10 * WIDTH
COL_GATES = 11 * WIDTH
SMALL_OFF = 10 * WIDTH

NT = (((1,), (1,)), ((), ()))
TN = (((0,), (0,)), ((), ()))


def _params(*sem):
    return pltpu.CompilerParams(dimension_semantics=sem, vmem_limit_bytes=VMEM_LIMIT)


def _dot(a, b):
    return jnp.dot(a.astype(BF16), b.astype(BF16), preferred_element_type=F32)


def _dot_nt(a, b):
    return lax.dot_general(a.astype(BF16), b.astype(BF16), NT, preferred_element_type=F32)


def _dot_tn(a, b):
    return lax.dot_general(a.astype(BF16), b.astype(BF16), TN, preferred_element_type=F32)


def _split3(a):
    hi = a.astype(BF16)
    r1 = a - hi.astype(F32)
    mid = r1.astype(BF16)
    lo = (r1 - mid.astype(F32)).astype(BF16)
    return hi, mid, lo


def _dot_exact_rhs(a, b01):
    return sum(jnp.dot(t, b01, preferred_element_type=F32) for t in _split3(a))


def _dot_exact_lhs(a01, b):
    return sum(jnp.dot(a01, t, preferred_element_type=F32) for t in _split3(b))


def _silu(x):
    return x * jax.nn.sigmoid(x)


def _rms(x):
    return x * lax.rsqrt(jnp.mean(x * x, axis=-1, keepdims=True) + EPS)


def _norm_small_kernel(x_ref, g_ref, ws_ref, wst_ref, h_ref, zs_ref, zst_ref):
    hb = (_rms(x_ref[...]) * g_ref[...]).astype(BF16)
    h_ref[...] = hb
    zs_ref[...] = jnp.dot(hb, ws_ref[...], preferred_element_type=F32)
    zst_ref[...] = lax.dot_general(wst_ref[...], hb, NT, preferred_element_type=F32)


def _norm_small(x2, gain, w_small, w_small_t, tm):
    n, d = x2.shape
    return pl.pallas_call(
        _norm_small_kernel,
        grid=(n // tm,),
        in_specs=[pl.BlockSpec((tm, d), lambda i: (i, 0)),
                  pl.BlockSpec((1, d), lambda i: (0, 0)),
                  pl.BlockSpec((d, LANES), lambda i: (0, 0)),
                  pl.BlockSpec((2 * HEADS, d), lambda i: (0, 0))],
        out_specs=[pl.BlockSpec((tm, d), lambda i: (i, 0)),
                   pl.BlockSpec((tm, LANES), lambda i: (i, 0)),
                   pl.BlockSpec((2 * HEADS, tm), lambda i: (0, i))],
        out_shape=[jax.ShapeDtypeStruct((n, d), BF16),
                   jax.ShapeDtypeStruct((n, LANES), F32),
                   jax.ShapeDtypeStruct((2 * HEADS, n), F32)],
        compiler_params=_params("parallel"),
        name="norm_small",
    )(x2, gain, w_small, w_small_t)


def _mm_two_part_kernel(a_ref, wh_ref, wt_ref, o_ref, *, n_head):
    j = pl.program_id(1)

    @pl.when(j < n_head)
    def _():
        o_ref[...] = jnp.dot(a_ref[...], wh_ref[...], preferred_element_type=F32).astype(o_ref.dtype)

    @pl.when(j >= n_head)
    def _():
        o_ref[...] = jnp.dot(a_ref[...], wt_ref[...], preferred_element_type=F32).astype(o_ref.dtype)


def _matmul_two_part(a, w_head, w_tail, tm, tn, out_dtype):
    m, k = a.shape
    n_head, n_tail = w_head.shape[1] // tn, w_tail.shape[1] // tn
    return pl.pallas_call(
        functools.partial(_mm_two_part_kernel, n_head=n_head),
        grid=(m // tm, n_head + n_tail),
        in_specs=[pl.BlockSpec((tm, k), lambda i, j: (i, 0)),
                  pl.BlockSpec((k, tn), lambda i, j: (0, jnp.minimum(j, n_head - 1))),
                  pl.BlockSpec((k, tn), lambda i, j: (0, jnp.maximum(j - n_head, 0)))],
        out_specs=pl.BlockSpec((tm, tn), lambda i, j: (i, j)),
        out_shape=jax.ShapeDtypeStruct((m, (n_head + n_tail) * tn), out_dtype),
        compiler_params=_params("parallel", "arbitrary"),
        name="matmul",
    )(a, w_head, w_tail)


def _mm_res_kernel(a_ref, w_ref, r_ref, o_ref):
    o_ref[...] = r_ref[...] + jnp.dot(a_ref[...], w_ref[...], preferred_element_type=F32)


def _matmul_residual(a, w, res, tm, tn):
    m, k = a.shape
    n = w.shape[1]
    return pl.pallas_call(
        _mm_res_kernel,
        grid=(m // tm, n // tn),
        in_specs=[pl.BlockSpec((tm, k), lambda i, j: (i, 0)),
                  pl.BlockSpec((k, tn), lambda i, j: (0, j)),
                  pl.BlockSpec((tm, tn), lambda i, j: (i, j))],
        out_specs=pl.BlockSpec((tm, tn), lambda i, j: (i, j)),
        out_shape=jax.ShapeDtypeStruct((m, n), F32),
        compiler_params=_params("parallel", "arbitrary"),
        name="matmul_residual",
    )(a, w, res)


def _ret_kernel(q_ref, k_ref, v_ref, g_ref, cos_ref, sin_ref, inner_ref, qh_ref, kt_ref, cd_ref,
                o_ref, s_ref, *, n_chunks):
    @pl.when(pl.program_id(1) == 0)
    def _():
        s_ref[...] = jnp.zeros_like(s_ref)

    scale = HEAD_DIM ** -0.5
    c = RET_CHUNK
    each = lambda fn, *lists: [fn(*args) for args in zip(*lists)]
    rot = lambda x, cos, sin: x * cos + pltpu.roll(x, HEAD_DIM // 2, 1) * sin
    for ci in range(n_chunks):
        rows = slice(ci * c, (ci + 1) * c)
        cos, sin = cos_ref[rows, :], sin_ref[rows, :]
        for h0 in range(0, HEADS, RET_HEAD_GROUP):
            hs = list(range(h0, h0 + RET_HEAD_GROUP))
            sls = [slice(hh * HEAD_DIM, (hh + 1) * HEAD_DIM) for hh in hs]
            qr = [rot(q_ref[rows, sl], cos, sin) for sl in sls]
            kr = [rot(k_ref[rows, sl], cos, sin) * scale for sl in sls]
            vb = [v_ref[rows, sl].astype(BF16) for sl in sls]
            state = [s_ref[hh] for hh in hs]
            scores = each(lambda a, b, hh: _dot_nt(a, b) * inner_ref[hh], qr, kr, hs)
            o_inner = each(_dot, scores, vb)
            o_cross = each(lambda a, s, hh: _dot(a * qh_ref[hh], s), qr, state, hs)
            kv = each(lambda a, b, hh: _dot_tn(a * kt_ref[hh], b), kr, vb, hs)
            for hh, s, d in zip(hs, state, kv):
                s_ref[hh] = s * cd_ref[hh][0:1, :] + d
            for sl, a, b in zip(sls, o_inner, o_cross):
                o_ref[rows, sl] = (_rms(a + b) * _silu(g_ref[rows, sl])).astype(BF16)


def _retention_tables(t):
    h = np.arange(HEADS, dtype=np.float64)
    log_gamma = np.log1p(-np.exp2(-5.0 - h))
    pos = np.arange(RET_CHUNK, dtype=np.float64)
    rel = pos[:, None] - pos[None, :]
    inner = np.where(rel >= 0, np.exp(np.maximum(rel, 0.0)[None] * log_gamma[:, None, None]), 0.0)
    k_tail = np.exp((RET_CHUNK - 1.0 - pos)[None, :] * log_gamma[:, None])
    q_head = np.exp((pos + 1.0)[None, :] * log_gamma[:, None])
    chunk_decay = np.exp(RET_CHUNK * log_gamma)
    half = HEAD_DIM // 2
    inv_freq = ROPE_BASE ** (-np.arange(half, dtype=np.float64) / half)
    ang = np.arange(t, dtype=np.float64)[:, None] * inv_freq[None, :]
    cos = np.concatenate([np.cos(ang), np.cos(ang)], axis=1)
    sin = np.concatenate([-np.sin(ang), np.sin(ang)], axis=1)
    bc = lambda a: np.broadcast_to(a[:, :, None], (HEADS, RET_CHUNK, HEAD_DIM))
    f = lambda a: jnp.asarray(np.ascontiguousarray(a), F32)
    return (f(cos), f(sin), f(inner), f(bc(q_head)), f(bc(k_tail)),
            f(np.broadcast_to(chunk_decay[:, None, None], (HEADS, SUBLANES, HEAD_DIM))))


def _retention(z, b, t, tb):
    n = b * t
    nt = t // tb
    cos, sin, inner, q_head, k_tail, chunk_decay = _retention_tables(t)
    col = lambda off: pl.BlockSpec((tb, WIDTH), lambda bi, i, off=off: (bi * nt + i, off // WIDTH))
    pos = pl.BlockSpec((tb, HEAD_DIM), lambda bi, i: (i, 0))
    head_tab = lambda r: pl.BlockSpec((HEADS, r, HEAD_DIM), lambda bi, i: (0, 0, 0))
    return pl.pallas_call(
        functools.partial(_ret_kernel, n_chunks=tb // RET_CHUNK),
        grid=(b, nt),
        in_specs=[col(COL_RQ), col(COL_RK), col(COL_RV), col(COL_RG), pos, pos,
                  head_tab(RET_CHUNK), head_tab(RET_CHUNK), head_tab(RET_CHUNK), head_tab(SUBLANES)],
        out_specs=pl.BlockSpec((tb, WIDTH), lambda bi, i: (bi * nt + i, 0)),
        out_shape=jax.ShapeDtypeStruct((n, WIDTH), BF16),
        scratch_shapes=[pltpu.VMEM((HEADS, HEAD_DIM, HEAD_DIM), F32)],
        compiler_params=_params("parallel", "arbitrary"),
        name="retention",
    )(z, z, z, z, cos, sin, inner, q_head, k_tail, chunk_decay)


def _split2(a):
    hi = a.astype(BF16)
    return hi, (a - hi.astype(F32)).astype(BF16)


def _sb_kernel(q_ref, k_ref, v_ref, qn_ref, kn_ref, u_ref, o_ref, kb_ref, vb_ref, acc_ref, carry_ref, *, prep_rows):
    i = pl.program_id(2)
    t = k_ref.shape[0]
    bq = q_ref.shape[0]
    heads = [slice(h * HEAD_DIM, (h + 1) * HEAD_DIM) for h in range(SB_HEADS)]

    @pl.when(i == 0)
    def _():
        def prep(c, carry):
            rows = pl.ds(pl.multiple_of(c * prep_rows, prep_rows), prep_rows)
            for sl in heads:
                kb_ref[rows, sl] = (_rms(k_ref[rows, sl]) * kn_ref[...]).astype(BF16)
            vb_ref[rows, :] = v_ref[rows, :].astype(BF16)
            return carry
        lax.fori_loop(0, t // prep_rows, prep, 0)

    qb = [(_rms(q_ref[:, sl]) * qn_ref[...] * HEAD_DIM ** -0.5).astype(BF16) for sl in heads]
    u = u_ref[...]
    each = lambda fn, *lists: [fn(*args) for args in zip(*lists)]
    d32 = lambda a, b: jnp.dot(a, b, preferred_element_type=F32)
    streams = [(h, far) for h in range(SB_HEADS) for far in (0, 1)]

    def tile_pair(j, diagonal):
        has_far = j >= 1
        starts = [pl.multiple_of(j * bq, bq), pl.multiple_of(jnp.maximum(j - 1, 0) * bq, bq)]
        z = [lax.dot_general(qb[h], kb_ref[pl.ds(starts[far], bq), heads[h]], NT, preferred_element_type=F32)
             for h, far in streams]
        log_not = each(lambda x: -(jnp.maximum(x, 0.0) + jnp.log(1.0 + jnp.exp(-jnp.abs(x)))), z)
        if diagonal:
            causal = lax.broadcasted_iota(jnp.int32, (bq, bq), 1) < lax.broadcasted_iota(jnp.int32, (bq, bq), 0)
        for n, (h, far) in enumerate(streams):
            if far:
                log_not[n] = jnp.where(has_far, log_not[n], 0.0)
            elif diagonal:
                log_not[n] = jnp.where(causal, log_not[n], 0.0)
        split = each(_split2, log_not)
        later = [d32(hi, u) + d32(lo, u) for hi, lo in split]
        total = each(lambda x: jnp.sum(x, axis=1, keepdims=True), log_not)
        logw = each(lambda a, b, c: a + b + c, log_not, z, later)
        tops = []
        for h in range(SB_HEADS):
            near, far = 2 * h, 2 * h + 1
            if diagonal:
                w_near = jnp.where(causal, jnp.exp(logw[near]), 0.0)
                before_far = total[near]
            else:
                old = carry_ref[h]
                w_near = jnp.exp(logw[near] + old)
                before_far = old + total[near]
            w_far = jnp.where(has_far, jnp.exp(logw[far] + before_far), 0.0)
            pv = (d32(w_near.astype(BF16), vb_ref[pl.ds(starts[0], bq), heads[h]])
                  + d32(w_far.astype(BF16), vb_ref[pl.ds(starts[1], bq), heads[h]]))
            acc_ref[:, heads[h]] = pv if diagonal else acc_ref[:, heads[h]] + pv
            carry = before_far + total[far]
            carry_ref[h] = carry
            tops.append(jnp.max(carry))
        return functools.reduce(jnp.maximum, tops)

    def live(state):
        return (state[0] >= 0) & (state[1] > SB_DEAD_CARRY)

    def step(state):
        return state[0] - 2, tile_pair(state[0], False)

    lax.while_loop(live, step, (i - 2, tile_pair(i, True)))
    o_ref[...] = acc_ref[...].astype(BF16)


def _stick_breaking(z, qn, kn, b, t):
    n = b * t
    bq = _tile(t, SB_TILE)
    nt = t // bq
    w = SB_HEADS * HEAD_DIM
    j = np.arange(bq)
    u = jnp.asarray(j[:, None] > j[None, :], BF16)
    seq = lambda off: pl.BlockSpec((t, w), lambda bi, g, i, off=off: (bi, off // w + g))
    return pl.pallas_call(
        functools.partial(_sb_kernel, prep_rows=bq),
        grid=(b, HEADS // SB_HEADS, nt),
        in_specs=[pl.BlockSpec((bq, w), lambda bi, g, i: (bi * nt + i, COL_SQ // w + g)),
                  seq(COL_SK), seq(COL_SV),
                  pl.BlockSpec((1, HEAD_DIM), lambda bi, g, i: (0, 0)),
                  pl.BlockSpec((1, HEAD_DIM), lambda bi, g, i: (0, 0)),
                  pl.BlockSpec((bq, bq), lambda bi, g, i: (0, 0))],
        out_specs=pl.BlockSpec((bq, w), lambda bi, g, i: (bi * nt + i, g)),
        out_shape=jax.ShapeDtypeStruct((n, WIDTH), BF16),
        scratch_shapes=[pltpu.VMEM((t, w), BF16), pltpu.VMEM((t, w), BF16),
                        pltpu.VMEM((bq, w), F32), pltpu.VMEM((SB_HEADS, bq, 1), F32)],
        compiler_params=_params("parallel", "parallel", "arbitrary"),
        name="stick_breaking",
    )(z, z, z, qn, kn, u)


def _gdn_kernel(xq_ref, xk_ref, xv_ref, gz_ref, cq_ref, ck_ref, cv_ref, zs_ref, gat_ref, alog_ref, dt_ref,
                alog_t_ref, dt_t_ref, cum_ref, cum_t_ref, gn_ref, o_ref, bq_ref, bk_ref, bv_ref, s_ref):
    tb = GDN_BLOCK
    c = GDN_CHUNK
    halo = SUBLANES

    @pl.when(pl.program_id(1) == 0)
    def _():
        s_ref[...] = jnp.zeros_like(s_ref)
        for buf in (bq_ref, bk_ref, bv_ref):
            buf[0:halo, :] = jnp.zeros((halo, WIDTH), F32)

    def conv(x_ref, cw_ref, buf):
        buf[halo:halo + tb, :] = x_ref[...]
        y = cw_ref[CONV_WIDTH - 1:CONV_WIDTH, :] * buf[halo:halo + tb, :]
        for tap in range(CONV_WIDTH - 1):
            back = CONV_WIDTH - 1 - tap
            y = y + cw_ref[tap:tap + 1, :] * buf[halo - back:halo - back + tb, :]
        buf[0:halo, :] = buf[tb:tb + halo, :]
        buf[halo:halo + tb, :] = _silu(y)

    conv(xq_ref, cq_ref, bq_ref)
    conv(xk_ref, ck_ref, bk_ref)
    conv(xv_ref, cv_ref, bv_ref)

    zs = zs_ref[...]
    beta_lanes = jax.nn.sigmoid(zs)
    g_lanes = -jnp.exp(alog_ref[...]) * jax.nn.softplus(zs + dt_ref[...])
    gc_lanes = _dot_exact_lhs(cum_ref[...], g_lanes)
    g_t = -jnp.exp(alog_t_ref[...]) * jax.nn.softplus(gat_ref[...] + dt_t_ref[...])
    gc_t = _dot_exact_rhs(g_t, cum_t_ref[...])

    ri = lax.broadcasted_iota(jnp.int32, (tb, tb), 0)
    ci = lax.broadcasted_iota(jnp.int32, (tb, tb), 1)
    same = (ri // c) == (ci // c)
    lower = same & (ri >= ci)
    strict = same & (ri > ci)
    scale = HEAD_DIM ** -0.5
    gn = gn_ref[...]

    def each(fn, *lists):
        return [fn(*args) for args in zip(*lists)]

    def wide(m):
        return functools.reduce(lambda x, y: x + y, [m[cc * c:(cc + 1) * c, :] for cc in range(tb // c)])

    def block_diag(m):
        return jnp.where(same, jnp.concatenate([m] * (tb // c), axis=0), 0.0)

    eye_wide = (lax.broadcasted_iota(jnp.int32, (c, tb), 0) == lax.broadcasted_iota(jnp.int32, (c, tb), 1) % c).astype(F32)

    for h0 in range(0, HEADS, GDN_HEAD_GROUP):
        hs = list(range(h0, h0 + GDN_HEAD_GROUP))
        sls = [slice(hh * HEAD_DIM, (hh + 1) * HEAD_DIM) for hh in hs]
        q = [bq_ref[halo:halo + tb, sl] for sl in sls]
        k = [bk_ref[halo:halo + tb, sl] for sl in sls]
        v = [bv_ref[halo:halo + tb, sl] for sl in sls]
        q = each(lambda x: x * lax.rsqrt(jnp.sum(x * x, axis=-1, keepdims=True) + EPS) * scale, q)
        k = each(lambda x: x * lax.rsqrt(jnp.sum(x * x, axis=-1, keepdims=True) + EPS), k)
        beta = [beta_lanes[:, hh:hh + 1] for hh in hs]
        gc = [gc_lanes[:, HEADS + hh:HEADS + hh + 1] for hh in hs]
        gc_rows = [gc_t[hh:hh + 1, :] for hh in hs]
        decay = each(lambda gi, gj: jnp.where(lower, jnp.exp(jnp.where(lower, gi - gj, 0.0)), 0.0), gc, gc_rows)
        k_beta = each(lambda x, y: x * y, k, beta)
        kk = each(_dot_nt, k_beta, k)
        a = each(lambda x, d: jnp.where(strict, x * d, 0.0), kk, decay)
        x = each(lambda y: eye_wide - wide(y), a)
        p = each(lambda y: _dot(wide(y), y), a)
        for step in range(5):
            p_bd = each(block_diag, p)
            x = each(lambda y, d: y + _dot(y, d), x, p_bd)
            if step < 4:
                p = each(_dot, p, p_bd)
        t_inv = each(block_diag, x)
        e_gc = each(jnp.exp, gc)
        rhs = each(lambda vv, bb, kb, eg: jnp.concatenate([vv * bb, kb * eg], axis=1), v, beta, k_beta, e_gc)
        uw = each(_dot, t_inv, rhs)
        qk = each(lambda y, d: jnp.where(lower, y * d, 0.0), each(_dot_nt, q, k), decay)
        q_dec = each(lambda y, eg: y * eg, q, e_gc)
        for cc in range(tb // c):
            rows = slice(cc * c, (cc + 1) * c)
            g_last = [g[cc * c + c - 1:cc * c + c, :] for g in gc]
            k_dec = each(lambda kx, gl, g: kx[rows] * jnp.exp(gl - g[rows]), k, g_last, gc)
            state = [s_ref[hh] for hh in hs]
            ws = each(lambda y, s: _dot(y[rows, HEAD_DIM:], s), uw, state)
            v_new = each(lambda y, d: y[rows, :HEAD_DIM] - d, uw, ws)
            o_cross = each(lambda y, s: _dot(y[rows], s), q_dec, state)
            o_intra = each(lambda y, vn: _dot(y[rows, cc * c:(cc + 1) * c], vn), qk, v_new)
            kv = each(_dot_tn, k_dec, v_new)
            for hh, s, gl, d in zip(hs, state, g_last, kv):
                s_ref[hh] = s * jnp.exp(gl) + d
            for sl, oc, oi in zip(sls, o_cross, o_intra):
                o = oc + oi
                o_ref[rows, sl] = (_rms(o) * gn * _silu(gz_ref[rows, sl])).astype(BF16)


def _gated_deltanet(z, zs, zst, conv_w, a_log, dt_bias, out_norm, b, t):
    n = b * t
    tb, c = GDN_BLOCK, GDN_CHUNK
    nt = t // tb
    idx = np.arange(tb)
    same = (idx[:, None] // c) == (idx[None, :] // c)
    cum = (same & (idx[:, None] >= idx[None, :])).astype(np.float32)
    cum, cum_t = jnp.asarray(cum, BF16), jnp.asarray(cum.T, BF16)
    lanes = lambda a1: jnp.pad(a1, (HEADS, LANES - 2 * HEADS))[None, :]
    rows = lambda a1: jnp.broadcast_to(a1[:, None], (HEADS, tb))

    def col(off):
        return pl.BlockSpec((tb, WIDTH), lambda bi, i, off=off: (bi * nt + i, off // WIDTH))

    def ccol(blk):
        return pl.BlockSpec((CONV_WIDTH, WIDTH), lambda bi, i, blk=blk: (0, blk))

    const = lambda shape: pl.BlockSpec(shape, lambda bi, i: (0,) * len(shape))
    return pl.pallas_call(
        _gdn_kernel,
        grid=(b, nt),
        in_specs=[col(COL_GQ), col(COL_GK), col(COL_GV), col(COL_GZ), ccol(0), ccol(1), ccol(2),
                  pl.BlockSpec((tb, LANES), lambda bi, i: (bi * nt + i, 0)),
                  pl.BlockSpec((HEADS, tb), lambda bi, i: (1, bi * nt + i)),
                  const((1, LANES)), const((1, LANES)), const((HEADS, tb)), const((HEADS, tb)),
                  const((tb, tb)), const((tb, tb)), const((1, HEAD_DIM))],
        out_specs=pl.BlockSpec((tb, WIDTH), lambda bi, i: (bi * nt + i, 0)),
        out_shape=jax.ShapeDtypeStruct((n, WIDTH), BF16),
        scratch_shapes=[pltpu.VMEM((tb + SUBLANES, WIDTH), F32)] * 3 + [pltpu.VMEM((HEADS, HEAD_DIM, HEAD_DIM), F32)],
        compiler_params=_params("parallel", "arbitrary"),
        name="gated_deltanet",
    )(z, z, z, z, conv_w, conv_w, conv_w, zs, zst, lanes(a_log), lanes(dt_bias), rows(a_log), rows(dt_bias),
      cum, cum_t, out_norm)


def _merge_kernel(or_ref, os_ref, og_ref, wb_ref, gr_ref, gs_ref, gg_ref, o_ref):
    acc = jax.nn.sigmoid(gr_ref[...]) * jnp.dot(or_ref[...], wb_ref[0], preferred_element_type=F32)
    acc = acc + jax.nn.sigmoid(gs_ref[...]) * jnp.dot(os_ref[...], wb_ref[1], preferred_element_type=F32)
    acc = acc + jax.nn.sigmoid(gg_ref[...]) * jnp.dot(og_ref[...], wb_ref[2], preferred_element_type=F32)
    o_ref[...] = acc.astype(BF16)


def _merge(o_r, o_s, o_g, w_branch, z, d, tm, tn):
    n = o_r.shape[0]
    assert COL_GATES % tn == 0 and d % tn == 0
    branch = pl.BlockSpec((tm, WIDTH), lambda j, i: (i, 0))
    gate = lambda bidx: pl.BlockSpec((tm, tn), lambda j, i, bidx=bidx: (i, (COL_GATES + bidx * d) // tn + j))
    return pl.pallas_call(
        _merge_kernel,
        grid=(d // tn, n // tm),
        in_specs=[branch, branch, branch,
                  pl.BlockSpec((3, WIDTH, tn), lambda j, i: (0, 0, j)),
                  gate(0), gate(1), gate(2)],
        out_specs=pl.BlockSpec((tm, tn), lambda j, i: (i, j)),
        out_shape=jax.ShapeDtypeStruct((n, d), BF16),
        compiler_params=_params("parallel", "parallel"),
        name="branch_merge",
    )(o_r, o_s, o_g, w_branch, z, z, z)


def _norm_router_kernel(x_ref, g_ref, wr_ref, h_ref, rt_ref):
    h = _rms(x_ref[...]) * g_ref[...]
    h_ref[...] = h
    (h_hi, h_lo), (w_hi, w_lo) = _split2(h), _split2(wr_ref[...])
    d32 = lambda a, b: jnp.dot(a, b, preferred_element_type=F32)
    logits = d32(h_hi, w_hi) + (d32(h_hi, w_lo) + d32(h_lo, w_hi))
    lane = lax.broadcasted_iota(jnp.int32, logits.shape, 1)
    neg = jnp.float32(-jnp.inf)
    lg = jnp.where(lane < N_EXPERTS, logits, neg)
    m1 = jnp.max(lg, axis=-1, keepdims=True)
    i1 = jnp.min(jnp.where(lg == m1, lane, LANES), axis=-1, keepdims=True)
    lg2 = jnp.where(lane == i1, neg, lg)
    m2 = jnp.max(lg2, axis=-1, keepdims=True)
    i2 = jnp.min(jnp.where(lg2 == m2, lane, LANES), axis=-1, keepdims=True)
    e = jnp.exp(m2 - m1)
    g1 = 1.0 / (1.0 + e)
    g2 = e / (1.0 + e)
    rt_ref[...] = jnp.where(lane == 0, i1.astype(F32),
                            jnp.where(lane == 1, i2.astype(F32),
                                      jnp.where(lane == 2, g1, jnp.where(lane == 3, g2, 0.0))))


def _norm_router(x2, gain, w_router_pad, tm):
    n, d = x2.shape
    return pl.pallas_call(
        _norm_router_kernel,
        grid=(n // tm,),
        in_specs=[pl.BlockSpec((tm, d), lambda i: (i, 0)), pl.BlockSpec((1, d), lambda i: (0, 0)),
                  pl.BlockSpec((d, LANES), lambda i: (0, 0))],
        out_specs=[pl.BlockSpec((tm, d), lambda i: (i, 0)), pl.BlockSpec((tm, LANES), lambda i: (i, 0))],
        out_shape=[jax.ShapeDtypeStruct((n, d), F32), jax.ShapeDtypeStruct((n, LANES), F32)],
        compiler_params=_params("parallel"),
        name="norm_router",
    )(x2, gain, w_router_pad)


def _dense_ffn_kernel(x_ref, g_ref, wg_ref, wu_ref, wd_ref, o_ref, xb_ref):
    @pl.when(pl.program_id(1) == 0)
    def _():
        x = x_ref[...]
        xb_ref[...] = (_rms(x) * g_ref[...]).astype(BF16)
        o_ref[...] = x

    xb = xb_ref[...]
    gate = jnp.dot(xb, wg_ref[...], preferred_element_type=F32)
    up = jnp.dot(xb, wu_ref[...], preferred_element_type=F32)
    act = (_silu(gate) * up).astype(BF16)
    o_ref[...] += jnp.dot(act, wd_ref[...], preferred_element_type=F32)


def _dense_ffn(x2, norm_w, w_gate, w_up, w_down, bm, tf):
    n, d = x2.shape
    d_ff = w_gate.shape[1]
    rows = pl.BlockSpec((bm, d), lambda i, f: (i, 0))
    return pl.pallas_call(
        _dense_ffn_kernel,
        grid=(n // bm, d_ff // tf),
        in_specs=[rows, pl.BlockSpec((1, d), lambda i, f: (0, 0)),
                  pl.BlockSpec((d, tf), lambda i, f: (0, f)), pl.BlockSpec((d, tf), lambda i, f: (0, f)),
                  pl.BlockSpec((tf, d), lambda i, f: (f, 0))],
        out_specs=rows,
        out_shape=jax.ShapeDtypeStruct((n, d), F32),
        scratch_shapes=[pltpu.VMEM((bm, d), BF16)],
        compiler_params=_params("parallel", "arbitrary"),
        name="dense_ffn",
    )(x2, norm_w[None, :], w_gate.astype(BF16), w_up.astype(BF16), w_down.astype(BF16))


def _row_copy(src_hbm, dst_ref, src_row, dst_row, sem):
    return pltpu.make_async_copy(src_hbm.at[pl.ds(src_row, 1), :], dst_ref.at[pl.ds(dst_row, 1), :], sem)


def _moe_swiglu_kernel(be_ref, nu_ref, tok_ref, h_hbm, wg_ref, wu_ref, wd_ref, o_ref, xg_ref, xb_ref, sem_ref,
                       *, bm, nf):
    blk, f = pl.program_id(0), pl.program_id(1)
    n_used = nu_ref[0]
    slot = blk % 2
    rows_per_step = -(-bm // nf)
    fetched = rows_per_step * nf

    def fetch(block, row, to_slot):
        return _row_copy(h_hbm, xg_ref.at[to_slot], tok_ref[block * bm + row], row, sem_ref.at[to_slot])

    def wait_fetched(of_slot):
        def wait(r, carry):
            _row_copy(h_hbm, xg_ref.at[of_slot], 0, r, sem_ref.at[of_slot]).wait()
            return carry
        lax.fori_loop(0, fetched, wait, 0, unroll=8)

    @pl.when((blk == 0) & (f == 0))
    def _():
        def start(r, carry):
            fetch(0, r, 0).start()
            return carry
        lax.fori_loop(0, fetched, start, 0)

    @pl.when(blk < n_used)
    def _():
        @pl.when(f == 0)
        def _():
            wait_fetched(slot)
            xb_ref[...] = xg_ref[slot, 0:bm, :].astype(BF16)
            o_ref[...] = jnp.zeros_like(o_ref)

        for r in range(rows_per_step):
            fetch(blk + 1, f * rows_per_step + r, 1 - slot).start()

        xb = xb_ref[...]
        gate = jnp.dot(xb, wg_ref[0], preferred_element_type=F32)
        up = jnp.dot(xb, wu_ref[0], preferred_element_type=F32)
        act = (_silu(gate) * up).astype(BF16)
        o_ref[...] += jnp.dot(act, wd_ref[0], preferred_element_type=F32)

    @pl.when((blk >= n_used) & (f == 0))
    def _():
        @pl.when(blk == n_used)
        def _():
            wait_fetched(slot)
        o_ref[...] = jnp.zeros_like(o_ref)


def _moe_swiglu(h, slot_tok, w_gate, w_up, w_down, block_expert, n_used, bm, tf):
    d = h.shape[1]
    d_ff = w_gate.shape[2]
    nb, nf = slot_tok.shape[0] // bm, d_ff // tf
    buffer_rows = -(-(-(-bm // nf) * nf) // SUBLANES) * SUBLANES

    def f_idx(blk, f, nu):
        return jnp.where(blk < nu[0], f, nf - 1)

    def w_in_map(blk, f, be, nu, tok):
        return (be[jnp.minimum(blk, nu[0] - 1)], 0, f_idx(blk, f, nu))

    def w_out_map(blk, f, be, nu, tok):
        return (be[jnp.minimum(blk, nu[0] - 1)], f_idx(blk, f, nu), 0)

    return pl.pallas_call(
        functools.partial(_moe_swiglu_kernel, bm=bm, nf=nf),
        grid_spec=pltpu.PrefetchScalarGridSpec(
            num_scalar_prefetch=3, grid=(nb, nf),
            in_specs=[pl.BlockSpec(memory_space=pl.ANY),
                      pl.BlockSpec((1, d, tf), w_in_map), pl.BlockSpec((1, d, tf), w_in_map),
                      pl.BlockSpec((1, tf, d), w_out_map)],
            out_specs=pl.BlockSpec((bm, d), lambda blk, f, be, nu, tok: (blk, 0)),
            scratch_shapes=[pltpu.VMEM((2, buffer_rows, d), F32), pltpu.VMEM((bm, d), BF16),
                            pltpu.SemaphoreType.DMA((2,))]),
        out_shape=jax.ShapeDtypeStruct((nb * bm, d), F32),
        compiler_params=_params("arbitrary", "arbitrary"),
        name="moe_swiglu",
    )(block_expert, n_used, slot_tok, h, w_gate, w_up, w_down)


def _combine_kernel(pos_ref, x_ref, rt_ref, y_hbm, o_ref, b0_ref, b1_ref, sem, *, tm):
    base = pl.program_id(0) * tm

    def start(r, carry):
        _row_copy(y_hbm, b0_ref, pos_ref[2 * (base + r)], r, sem).start()
        _row_copy(y_hbm, b1_ref, pos_ref[2 * (base + r) + 1], r, sem).start()
        return carry

    def wait(r, carry):
        _row_copy(y_hbm, b0_ref, 0, r, sem).wait()
        _row_copy(y_hbm, b1_ref, 0, r, sem).wait()
        return carry

    lax.fori_loop(0, tm, start, 0, unroll=8)
    lax.fori_loop(0, tm, wait, 0, unroll=8)
    rt = rt_ref[...]
    o_ref[...] = x_ref[...] + (b0_ref[...] * rt[:, 2:3] + b1_ref[...] * rt[:, 3:4])


def _combine(x2, route, y_sorted, pos, tm):
    n, d = x2.shape
    return pl.pallas_call(
        functools.partial(_combine_kernel, tm=tm),
        grid_spec=pltpu.PrefetchScalarGridSpec(
            num_scalar_prefetch=1, grid=(n // tm,),
            in_specs=[pl.BlockSpec((tm, d), lambda i, p: (i, 0)),
                      pl.BlockSpec((tm, LANES), lambda i, p: (i, 0)),
                      pl.BlockSpec(memory_space=pl.ANY)],
            out_specs=pl.BlockSpec((tm, d), lambda i, p: (i, 0)),
            scratch_shapes=[pltpu.VMEM((tm, d), F32), pltpu.VMEM((tm, d), F32), pltpu.SemaphoreType.DMA(())]),
        out_shape=jax.ShapeDtypeStruct((n, d), F32),
        compiler_params=_params("arbitrary"),
        name="combine",
    )(pos, x2, route, y_sorted)


def _tile(n, want):
    for cand in range(min(n, want), 0, -LANES):
        if n % cand == 0:
            return cand
    raise ValueError((n, want))


def _mixer(x2, b, t, norm_w, w_in, sb_qn, sb_kn, conv_w, a_log, dt_bias, gdn_on, w_branch, w_out):
    n, d = x2.shape
    w_head = w_in[:, :SMALL_OFF].astype(BF16)
    w_tail = w_in[:, SMALL_OFF + 2 * HEADS:].astype(BF16)
    w_small = w_in[:, SMALL_OFF:SMALL_OFF + 2 * HEADS]
    w_small_pad = jnp.pad(w_small, ((0, 0), (0, LANES - 2 * HEADS))).astype(BF16)
    hidden, zs, zst = _norm_small(x2, norm_w[None, :], w_small_pad, w_small.T.astype(BF16), _tile(n, 512))
    z = _matmul_two_part(hidden, w_head, w_tail, _tile(n, 1024),
                         _tile(math.gcd(w_head.shape[1], w_tail.shape[1]), 1024), F32)
    o_r = _retention(z, b, t, _tile(t, 512))
    o_s = _stick_breaking(z, sb_qn[None, :], sb_kn[None, :], b, t)
    o_g = _gated_deltanet(z, zs, zst, conv_w, a_log, dt_bias, gdn_on[None, :], b, t)
    merged = _merge(o_r, o_s, o_g, w_branch.astype(BF16), z, d, _tile(n, 512), _tile(d, 1024))
    return _matmul_residual(merged, w_out.astype(BF16), x2, _tile(n, 1024), _tile(d, 1024))


def _moe_ffn(x2, norm_w, w_router, w_gate, w_up, w_down, bm, tf):
    n, d = x2.shape
    h, route = _norm_router(x2, norm_w[None, :], jnp.pad(w_router, ((0, 0), (0, LANES - N_EXPERTS))),
                            _tile(n, 512))
    flat_e = route[:, :TOP_K].astype(jnp.int32).reshape(-1)
    n_assign = n * TOP_K
    onehot = (flat_e[:, None] == jnp.arange(N_EXPERTS, dtype=jnp.int32)[None, :]).astype(jnp.int32)
    running = jnp.cumsum(onehot, axis=0)
    counts = running[-1]
    rank = jnp.sum((running - onehot) * onehot, axis=1)
    padded = (counts + bm - 1) // bm * bm
    pad_end = jnp.cumsum(padded)
    pad_start = pad_end - padded
    dest = (pad_start[flat_e] + rank).astype(jnp.int32)
    n_blocks = -(-n_assign // bm) + N_EXPERTS
    n_slots = n_blocks * bm
    flat_tok = jnp.arange(n_assign, dtype=jnp.int32) // TOP_K
    slot_tok = jnp.zeros((n_slots,), jnp.int32).at[dest].set(flat_tok)
    block_e = jnp.minimum(jnp.searchsorted(pad_end, jnp.arange(n_blocks, dtype=jnp.int32) * bm, side='right'),
                          N_EXPERTS - 1).astype(jnp.int32)
    n_used = (pad_end[-1:] // bm).astype(jnp.int32)
    y_sorted = _moe_swiglu(h, slot_tok, w_gate.astype(BF16), w_up.astype(BF16), w_down.astype(BF16),
                           block_e, n_used, bm, tf)
    return _combine(x2, route, y_sorted, dest, _tile(n, 256))


def kernel(x, mix_norm, w_in, sb_q_norm, sb_k_norm, gdn_conv, gdn_a_log, gdn_dt_bias, gdn_out_norm, w_branch,
           w_out, ffn_norm, w_ffn_gate, w_ffn_up, w_ffn_down, w_router, w_exp_gate, w_exp_up, w_exp_down):
    b, t, d = x.shape
    depth = mix_norm.shape[0]
    d_ff = w_ffn_gate.shape[2]
    x2 = x.reshape(b * t, d)
    bm = _tile(b * t, 512)
    tf = _tile(d_ff, 512)
    for layer in range(depth):
        x2 = _mixer(x2, b, t, mix_norm[layer], w_in[layer], sb_q_norm[layer], sb_k_norm[layer], gdn_conv[layer],
                    gdn_a_log[layer], gdn_dt_bias[layer], gdn_out_norm[layer], w_branch[layer], w_out[layer])
        i = layer // 2
        if layer % 2 == 0:
            x2 = _dense_ffn(x2, ffn_norm[layer], w_ffn_gate[i], w_ffn_up[i], w_ffn_down[i],
                            _tile(b * t, 2 * bm), tf)
        else:
            x2 = _moe_ffn(x2, ffn_norm[layer], w_router[i], w_exp_gate[i], w_exp_up[i], w_exp_down[i], bm, tf)
    return x2.reshape(b, t, d)
```

```python
import functools

import numpy as np
import jax
import jax.numpy as jnp
from jax import lax
from jax.experimental import pallas as pl
from jax.experimental.pallas import tpu as pltpu

F32 = jnp.float32
BF16 = jnp.bfloat16

HEAD_DIM = 128
HEADS = 8
WIDTH = HEADS * HEAD_DIM
RET_CHUNK = 128
RET_HEAD_GROUP = 4
SB_TILE = 256
SB_HEADS = 4
SB_DEAD_CARRY = -104.0
GDN_CHUNK = 64
GDN_BLOCK = 256
GDN_HEAD_GROUP = 8
CONV_WIDTH = 4
ROPE_BASE = 10000.0
N_EXPERTS = 8
TOP_K = 2
EPS = 1e-6
LANES = 128
SUBLANES = 8
VMEM_LIMIT = 56 * 1024 * 1024

COL_RQ, COL_RK, COL_RV, COL_RG = 0, WIDTH, 2 * WIDTH, 3 * WIDTH
COL_SQ, COL_SK, COL_SV = 4 * WIDTH, 5 * WIDTH, 6 * WIDTH
COL_GQ, COL_GK, COL_GV, COL_GZ = 7 * WIDTH, 8 * WIDTH, 9 * WIDTH, 10 * WIDTH
COL_GATES = 11 * WIDTH
SMALL_OFF = 10 * WIDTH

NT = (((1,), (1,)), ((), ()))
TN = (((0,), (0,)), ((), ()))


def _params(*sem):
    return pltpu.CompilerParams(dimension_semantics=sem, vmem_limit_bytes=VMEM_LIMIT)


def _dot(a, b):
    return jnp.dot(a.astype(BF16), b.astype(BF16), preferred_element_type=F32)


def _dot_nt(a, b):
    return lax.dot_general(a.astype(BF16), b.astype(BF16), NT, preferred_element_type=F32)


def _dot_tn(a, b):
    return lax.dot_general(a.astype(BF16), b.astype(BF16), TN, preferred_element_type=F32)


def _split3(a):
    hi = a.astype(BF16)
    r1 = a - hi.astype(F32)
    mid = r1.astype(BF16)
    lo = (r1 - mid.astype(F32)).astype(BF16)
    return hi, mid, lo


def _dot_exact_rhs(a, b01):
    return sum(jnp.dot(t, b01, preferred_element_type=F32) for t in _split3(a))


def _dot_exact_lhs(a01, b):
    return sum(jnp.dot(a01, t, preferred_element_type=F32) for t in _split3(b))


def _silu(x):
    return x * jax.nn.sigmoid(x)


def _rms(x):
    return x * lax.rsqrt(jnp.mean(x * x, axis=-1, keepdims=True) + EPS)


def _norm_small_kernel(x_ref, g_ref, ws_ref, wst_ref, h_ref, zs_ref, zst_ref):
    hb = (_rms(x_ref[...]) * g_ref[...]).astype(BF16)
    h_ref[...] = hb
    zs_ref[...] = jnp.dot(hb, ws_ref[...], preferred_element_type=F32)
    zst_ref[...] = lax.dot_general(wst_ref[...], hb, NT, preferred_element_type=F32)


def _norm_small(x2, gain, w_small, w_small_t, tm):
    n, d = x2.shape
    return pl.pallas_call(
        _norm_small_kernel,
        grid=(n // tm,),
        in_specs=[pl.BlockSpec((tm, d), lambda i: (i, 0)),
                  pl.BlockSpec((1, d), lambda i: (0, 0)),
                  pl.BlockSpec((d, LANES), lambda i: (0, 0)),
                  pl.BlockSpec((2 * HEADS, d), lambda i: (0, 0))],
        out_specs=[pl.BlockSpec((tm, d), lambda i: (i, 0)),
                   pl.BlockSpec((tm, LANES), lambda i: (i, 0)),
                   pl.BlockSpec((2 * HEADS, tm), lambda i: (0, i))],
        out_shape=[jax.ShapeDtypeStruct((n, d), BF16),
                   jax.ShapeDtypeStruct((n, LANES), F32),
                   jax.ShapeDtypeStruct((2 * HEADS, n), F32)],
        compiler_params=_params("parallel"),
        name="norm_small",
    )(x2, gain, w_small, w_small_t)


def _mm_kernel(a_ref, w_ref, o_ref):
    o_ref[...] = jnp.dot(a_ref[...], w_ref[...], preferred_element_type=F32).astype(o_ref.dtype)


def _matmul(a, w, tm, tn, out_dtype):
    m, k = a.shape
    n = w.shape[1]
    return pl.pallas_call(
        _mm_kernel,
        grid=(m // tm, n // tn),
        in_specs=[pl.BlockSpec((tm, k), lambda i, j: (i, 0)),
                  pl.BlockSpec((k, tn), lambda i, j: (0, j))],
        out_specs=pl.BlockSpec((tm, tn), lambda i, j: (i, j)),
        out_shape=jax.ShapeDtypeStruct((m, n), out_dtype),
        compiler_params=_params("parallel", "arbitrary"),
        name="matmul",
    )(a, w)


def _mm_res_kernel(a_ref, w_ref, r_ref, o_ref):
    o_ref[...] = r_ref[...] + jnp.dot(a_ref[...], w_ref[...], preferred_element_type=F32)


def _matmul_residual(a, w, res, tm, tn):
    m, k = a.shape
    n = w.shape[1]
    return pl.pallas_call(
        _mm_res_kernel,
        grid=(m // tm, n // tn),
        in_specs=[pl.BlockSpec((tm, k), lambda i, j: (i, 0)),
                  pl.BlockSpec((k, tn), lambda i, j: (0, j)),
                  pl.BlockSpec((tm, tn), lambda i, j: (i, j))],
        out_specs=pl.BlockSpec((tm, tn), lambda i, j: (i, j)),
        out_shape=jax.ShapeDtypeStruct((m, n), F32),
        compiler_params=_params("parallel", "arbitrary"),
        name="matmul_residual",
    )(a, w, res)


def _ret_kernel(q_ref, k_ref, v_ref, g_ref, cos_ref, sin_ref, inner_ref, qh_ref, kt_ref, cd_ref,
                o_ref, s_ref, *, n_chunks):
    @pl.when(pl.program_id(1) == 0)
    def _():
        s_ref[...] = jnp.zeros_like(s_ref)

    scale = HEAD_DIM ** -0.5
    c = RET_CHUNK
    each = lambda fn, *lists: [fn(*args) for args in zip(*lists)]
    rot = lambda x, cos, sin: x * cos + pltpu.roll(x, HEAD_DIM // 2, 1) * sin
    for ci in range(n_chunks):
        rows = slice(ci * c, (ci + 1) * c)
        cos, sin = cos_ref[rows, :], sin_ref[rows, :]
        for h0 in range(0, HEADS, RET_HEAD_GROUP):
            hs = list(range(h0, h0 + RET_HEAD_GROUP))
            sls = [slice(hh * HEAD_DIM, (hh + 1) * HEAD_DIM) for hh in hs]
            qr = [rot(q_ref[rows, sl], cos, sin) for sl in sls]
            kr = [rot(k_ref[rows, sl], cos, sin) * scale for sl in sls]
            vb = [v_ref[rows, sl].astype(BF16) for sl in sls]
            state = [s_ref[hh] for hh in hs]
            scores = each(lambda a, b, hh: _dot_nt(a, b) * inner_ref[hh], qr, kr, hs)
            o_inner = each(_dot, scores, vb)
            o_cross = each(lambda a, s, hh: _dot(a * qh_ref[hh], s), qr, state, hs)
            kv = each(lambda a, b, hh: _dot_tn(a * kt_ref[hh], b), kr, vb, hs)
            for hh, s, d in zip(hs, state, kv):
                s_ref[hh] = s * cd_ref[hh][0:1, :] + d
            for sl, a, b in zip(sls, o_inner, o_cross):
                o_ref[rows, sl] = (_rms(a + b) * _silu(g_ref[rows, sl])).astype(BF16)


def _retention_tables(t):
    h = np.arange(HEADS, dtype=np.float64)
    log_gamma = np.log1p(-np.exp2(-5.0 - h))
    pos = np.arange(RET_CHUNK, dtype=np.float64)
    rel = pos[:, None] - pos[None, :]
    inner = np.where(rel >= 0, np.exp(np.maximum(rel, 0.0)[None] * log_gamma[:, None, None]), 0.0)
    k_tail = np.exp((RET_CHUNK - 1.0 - pos)[None, :] * log_gamma[:, None])
    q_head = np.exp((pos + 1.0)[None, :] * log_gamma[:, None])
    chunk_decay = np.exp(RET_CHUNK * log_gamma)
    half = HEAD_DIM // 2
    inv_freq = ROPE_BASE ** (-np.arange(half, dtype=np.float64) / half)
    ang = np.arange(t, dtype=np.float64)[:, None] * inv_freq[None, :]
    cos = np.concatenate([np.cos(ang), np.cos(ang)], axis=1)
    sin = np.concatenate([-np.sin(ang), np.sin(ang)], axis=1)
    bc = lambda a: np.broadcast_to(a[:, :, None], (HEADS, RET_CHUNK, HEAD_DIM))
    f = lambda a: jnp.asarray(np.ascontiguousarray(a), F32)
    return (f(cos), f(sin), f(inner), f(bc(q_head)), f(bc(k_tail)),
            f(np.broadcast_to(chunk_decay[:, None, None], (HEADS, SUBLANES, HEAD_DIM))))


def _retention(z, b, t, tb):
    n = b * t
    nt = t // tb
    cos, sin, inner, q_head, k_tail, chunk_decay = _retention_tables(t)
    col = lambda off: pl.BlockSpec((tb, WIDTH), lambda bi, i, off=off: (bi * nt + i, off // WIDTH))
    pos = pl.BlockSpec((tb, HEAD_DIM), lambda bi, i: (i, 0))
    head_tab = lambda r: pl.BlockSpec((HEADS, r, HEAD_DIM), lambda bi, i: (0, 0, 0))
    return pl.pallas_call(
        functools.partial(_ret_kernel, n_chunks=tb // RET_CHUNK),
        grid=(b, nt),
        in_specs=[col(COL_RQ), col(COL_RK), col(COL_RV), col(COL_RG), pos, pos,
                  head_tab(RET_CHUNK), head_tab(RET_CHUNK), head_tab(RET_CHUNK), head_tab(SUBLANES)],
        out_specs=pl.BlockSpec((tb, WIDTH), lambda bi, i: (bi * nt + i, 0)),
        out_shape=jax.ShapeDtypeStruct((n, WIDTH), BF16),
        scratch_shapes=[pltpu.VMEM((HEADS, HEAD_DIM, HEAD_DIM), F32)],
        compiler_params=_params("parallel", "arbitrary"),
        name="retention",
    )(z, z, z, z, cos, sin, inner, q_head, k_tail, chunk_decay)


def _split2(a):
    hi = a.astype(BF16)
    return hi, (a - hi.astype(F32)).astype(BF16)


def _sb_kernel(q_ref, k_ref, v_ref, qn_ref, kn_ref, u_ref, o_ref, kb_ref, vb_ref, acc_ref, carry_ref, *, prep_rows):
    i = pl.program_id(2)
    t = k_ref.shape[0]
    bq = q_ref.shape[0]
    heads = [slice(h * HEAD_DIM, (h + 1) * HEAD_DIM) for h in range(SB_HEADS)]

    @pl.when(i == 0)
    def _():
        def prep(c, carry):
            rows = pl.ds(pl.multiple_of(c * prep_rows, prep_rows), prep_rows)
            for sl in heads:
                kb_ref[rows, sl] = (_rms(k_ref[rows, sl]) * kn_ref[...]).astype(BF16)
            vb_ref[rows, :] = v_ref[rows, :].astype(BF16)
            return carry
        lax.fori_loop(0, t // prep_rows, prep, 0)

    qb = [(_rms(q_ref[:, sl]) * qn_ref[...] * HEAD_DIM ** -0.5).astype(BF16) for sl in heads]
    u = u_ref[...]
    each = lambda fn, *lists: [fn(*args) for args in zip(*lists)]
    d32 = lambda a, b: jnp.dot(a, b, preferred_element_type=F32)
    streams = [(h, far) for h in range(SB_HEADS) for far in (0, 1)]

    def tile_pair(j, diagonal):
        has_far = j >= 1
        starts = [pl.multiple_of(j * bq, bq), pl.multiple_of(jnp.maximum(j - 1, 0) * bq, bq)]
        z = [lax.dot_general(qb[h], kb_ref[pl.ds(starts[far], bq), heads[h]], NT, preferred_element_type=F32)
             for h, far in streams]
        log_not = each(lambda x: -(jnp.maximum(x, 0.0) + jnp.log(1.0 + jnp.exp(-jnp.abs(x)))), z)
        if diagonal:
            causal = lax.broadcasted_iota(jnp.int32, (bq, bq), 1) < lax.broadcasted_iota(jnp.int32, (bq, bq), 0)
        for n, (h, far) in enumerate(streams):
            if far:
                log_not[n] = jnp.where(has_far, log_not[n], 0.0)
            elif diagonal:
                log_not[n] = jnp.where(causal, log_not[n], 0.0)
        split = each(_split2, log_not)
        later = [d32(hi, u) + d32(lo, u) for hi, lo in split]
        total = each(lambda x: jnp.sum(x, axis=1, keepdims=True), log_not)
        logw = each(lambda a, b, c: a + b + c, log_not, z, later)
        tops = []
        for h in range(SB_HEADS):
            near, far = 2 * h, 2 * h + 1
            if diagonal:
                w_near = jnp.where(causal, jnp.exp(logw[near]), 0.0)
                before_far = total[near]
            else:
                old = carry_ref[h]
                w_near = jnp.exp(logw[near] + old)
                before_far = old + total[near]
            w_far = jnp.where(has_far, jnp.exp(logw[far] + before_far), 0.0)
            pv = (d32(w_near.astype(BF16), vb_ref[pl.ds(starts[0], bq), heads[h]])
                  + d32(w_far.astype(BF16), vb_ref[pl.ds(starts[1], bq), heads[h]]))
            acc_ref[:, heads[h]] = pv if diagonal else acc_ref[:, heads[h]] + pv
            carry = before_far + total[far]
            carry_ref[h] = carry
            tops.append(jnp.max(carry))
        return functools.reduce(jnp.maximum, tops)

    def live(state):
        return (state[0] >= 0) & (state[1] > SB_DEAD_CARRY)

    def step(state):
        return state[0] - 2, tile_pair(state[0], False)

    lax.while_loop(live, step, (i - 2, tile_pair(i, True)))
    o_ref[...] = acc_ref[...].astype(BF16)


def _stick_breaking(z, qn, kn, b, t):
    n = b * t
    bq = _tile(t, SB_TILE)
    nt = t // bq
    w = SB_HEADS * HEAD_DIM
    j = np.arange(bq)
    u = jnp.asarray(j[:, None] > j[None, :], BF16)
    seq = lambda off: pl.BlockSpec((t, w), lambda bi, g, i, off=off: (bi, off // w + g))
    return pl.pallas_call(
        functools.partial(_sb_kernel, prep_rows=bq),
        grid=(b, HEADS // SB_HEADS, nt),
        in_specs=[pl.BlockSpec((bq, w), lambda bi, g, i: (bi * nt + i, COL_SQ // w + g)),
                  seq(COL_SK), seq(COL_SV),
                  pl.BlockSpec((1, HEAD_DIM), lambda bi, g, i: (0, 0)),
                  pl.BlockSpec((1, HEAD_DIM), lambda bi, g, i: (0, 0)),
                  pl.BlockSpec((bq, bq), lambda bi, g, i: (0, 0))],
        out_specs=pl.BlockSpec((bq, w), lambda bi, g, i: (bi * nt + i, g)),
        out_shape=jax.ShapeDtypeStruct((n, WIDTH), BF16),
        scratch_shapes=[pltpu.VMEM((t, w), BF16), pltpu.VMEM((t, w), BF16),
                        pltpu.VMEM((bq, w), F32), pltpu.VMEM((SB_HEADS, bq, 1), F32)],
        compiler_params=_params("parallel", "parallel", "arbitrary"),
        name="stick_breaking",
    )(z, z, z, qn, kn, u)


def _gdn_kernel(xq_ref, xk_ref, xv_ref, gz_ref, cq_ref, ck_ref, cv_ref, zs_ref, gat_ref, alog_ref, dt_ref,
                alog_t_ref, dt_t_ref, cum_ref, cum_t_ref, gn_ref, o_ref, bq_ref, bk_ref, bv_ref, s_ref):
    tb = GDN_BLOCK
    c = GDN_CHUNK
    halo = SUBLANES

    @pl.when(pl.program_id(1) == 0)
    def _():
        s_ref[...] = jnp.zeros_like(s_ref)
        for buf in (bq_ref, bk_ref, bv_ref):
            buf[0:halo, :] = jnp.zeros((halo, WIDTH), F32)

    def conv(x_ref, cw_ref, buf):
        buf[halo:halo + tb, :] = x_ref[...]
        y = cw_ref[CONV_WIDTH - 1:CONV_WIDTH, :] * buf[halo:halo + tb, :]
        for tap in range(CONV_WIDTH - 1):
            back = CONV_WIDTH - 1 - tap
            y = y + cw_ref[tap:tap + 1, :] * buf[halo - back:halo - back + tb, :]
        buf[0:halo, :] = buf[tb:tb + halo, :]
        buf[halo:halo + tb, :] = _silu(y)

    conv(xq_ref, cq_ref, bq_ref)
    conv(xk_ref, ck_ref, bk_ref)
    conv(xv_ref, cv_ref, bv_ref)

    zs = zs_ref[...]
    beta_lanes = jax.nn.sigmoid(zs)
    g_lanes = -jnp.exp(alog_ref[...]) * jax.nn.softplus(zs + dt_ref[...])
    gc_lanes = _dot_exact_lhs(cum_ref[...], g_lanes)
    g_t = -jnp.exp(alog_t_ref[...]) * jax.nn.softplus(gat_ref[...] + dt_t_ref[...])
    gc_t = _dot_exact_rhs(g_t, cum_t_ref[...])

    ri = lax.broadcasted_iota(jnp.int32, (tb, tb), 0)
    ci = lax.broadcasted_iota(jnp.int32, (tb, tb), 1)
    same = (ri // c) == (ci // c)
    lower = same & (ri >= ci)
    strict = same & (ri > ci)
    scale = HEAD_DIM ** -0.5
    gn = gn_ref[...]

    def each(fn, *lists):
        return [fn(*args) for args in zip(*lists)]

    def wide(m):
        return functools.reduce(lambda x, y: x + y, [m[cc * c:(cc + 1) * c, :] for cc in range(tb // c)])

    def block_diag(m):
        return jnp.where(same, jnp.concatenate([m] * (tb // c), axis=0), 0.0)

    eye_wide = (lax.broadcasted_iota(jnp.int32, (c, tb), 0) == lax.broadcasted_iota(jnp.int32, (c, tb), 1) % c).astype(F32)

    for h0 in range(0, HEADS, GDN_HEAD_GROUP):
        hs = list(range(h0, h0 + GDN_HEAD_GROUP))
        sls = [slice(hh * HEAD_DIM, (hh + 1) * HEAD_DIM) for hh in hs]
        q = [bq_ref[halo:halo + tb, sl] for sl in sls]
        k = [bk_ref[halo:halo + tb, sl] for sl in sls]
        v = [bv_ref[halo:halo + tb, sl] for sl in sls]
        q = each(lambda x: x * lax.rsqrt(jnp.sum(x * x, axis=-1, keepdims=True) + EPS) * scale, q)
        k = each(lambda x: x * lax.rsqrt(jnp.sum(x * x, axis=-1, keepdims=True) + EPS), k)
        beta = [beta_lanes[:, hh:hh + 1] for hh in hs]
        gc = [gc_lanes[:, HEADS + hh:HEADS + hh + 1] for hh in hs]
        gc_rows = [gc_t[hh:hh + 1, :] for hh in hs]
        decay = each(lambda gi, gj: jnp.where(lower, jnp.exp(jnp.where(lower, gi - gj, 0.0)), 0.0), gc, gc_rows)
        k_beta = each(lambda x, y: x * y, k, beta)
        kk = each(_dot_nt, k_beta, k)
        a = each(lambda x, d: jnp.where(strict, x * d, 0.0), kk, decay)
        x = each(lambda y: eye_wide - wide(y), a)
        p = each(lambda y: _dot(wide(y), y), a)
        for step in range(5):
            p_bd = each(block_diag, p)
            x = each(lambda y, d: y + _dot(y, d), x, p_bd)
            if step < 4:
                p = each(_dot, p, p_bd)
        t_inv = each(block_diag, x)
        e_gc = each(jnp.exp, gc)
        rhs = each(lambda vv, bb, kb, eg: jnp.concatenate([vv * bb, kb * eg], axis=1), v, beta, k_beta, e_gc)
        uw = each(_dot, t_inv, rhs)
        qk = each(lambda y, d: jnp.where(lower, y * d, 0.0), each(_dot_nt, q, k), decay)
        q_dec = each(lambda y, eg: y * eg, q, e_gc)
        for cc in range(tb // c):
            rows = slice(cc * c, (cc + 1) * c)
            g_last = [g[cc * c + c - 1:cc * c + c, :] for g in gc]
            k_dec = each(lambda kx, gl, g: kx[rows] * jnp.exp(gl - g[rows]), k, g_last, gc)
            state = [s_ref[hh] for hh in hs]
            ws = each(lambda y, s: _dot(y[rows, HEAD_DIM:], s), uw, state)
            v_new = each(lambda y, d: y[rows, :HEAD_DIM] - d, uw, ws)
            o_cross = each(lambda y, s: _dot(y[rows], s), q_dec, state)
            o_intra = each(lambda y, vn: _dot(y[rows, cc * c:(cc + 1) * c], vn), qk, v_new)
            kv = each(_dot_tn, k_dec, v_new)
            for hh, s, gl, d in zip(hs, state, g_last, kv):
                s_ref[hh] = s * jnp.exp(gl) + d
            for sl, oc, oi in zip(sls, o_cross, o_intra):
                o = oc + oi
                o_ref[rows, sl] = (_rms(o) * gn * _silu(gz_ref[rows, sl])).astype(BF16)


def _gated_deltanet(z, zs, zst, conv_w, a_log, dt_bias, out_norm, b, t):
    n = b * t
    tb, c = GDN_BLOCK, GDN_CHUNK
    nt = t // tb
    idx = np.arange(tb)
    same = (idx[:, None] // c) == (idx[None, :] // c)
    cum = (same & (idx[:, None] >= idx[None, :])).astype(np.float32)
    cum, cum_t = jnp.asarray(cum, BF16), jnp.asarray(cum.T, BF16)
    lanes = lambda a1: jnp.pad(a1, (HEADS, LANES - 2 * HEADS))[None, :]
    rows = lambda a1: jnp.broadcast_to(a1[:, None], (HEADS, tb))

    def col(off):
        return pl.BlockSpec((tb, WIDTH), lambda bi, i, off=off: (bi * nt + i, off // WIDTH))

    def ccol(blk):
        return pl.BlockSpec((CONV_WIDTH, WIDTH), lambda bi, i, blk=blk: (0, blk))

    const = lambda shape: pl.BlockSpec(shape, lambda bi, i: (0,) * len(shape))
    return pl.pallas_call(
        _gdn_kernel,
        grid=(b, nt),
        in_specs=[col(COL_GQ), col(COL_GK), col(COL_GV), col(COL_GZ), ccol(0), ccol(1), ccol(2),
                  pl.BlockSpec((tb, LANES), lambda bi, i: (bi * nt + i, 0)),
                  pl.BlockSpec((HEADS, tb), lambda bi, i: (1, bi * nt + i)),
                  const((1, LANES)), const((1, LANES)), const((HEADS, tb)), const((HEADS, tb)),
                  const((tb, tb)), const((tb, tb)), const((1, HEAD_DIM))],
        out_specs=pl.BlockSpec((tb, WIDTH), lambda bi, i: (bi * nt + i, 0)),
        out_shape=jax.ShapeDtypeStruct((n, WIDTH), BF16),
        scratch_shapes=[pltpu.VMEM((tb + SUBLANES, WIDTH), F32)] * 3 + [pltpu.VMEM((HEADS, HEAD_DIM, HEAD_DIM), F32)],
        compiler_params=_params("parallel", "arbitrary"),
        name="gated_deltanet",
    )(z, z, z, z, conv_w, conv_w, conv_w, zs, zst, lanes(a_log), lanes(dt_bias), rows(a_log), rows(dt_bias),
      cum, cum_t, out_norm)


def _merge_kernel(or_ref, os_ref, og_ref, wb_ref, gr_ref, gs_ref, gg_ref, o_ref):
    acc = jax.nn.sigmoid(gr_ref[...]) * jnp.dot(or_ref[...], wb_ref[0], preferred_element_type=F32)
    acc = acc + jax.nn.sigmoid(gs_ref[...]) * jnp.dot(os_ref[...], wb_ref[1], preferred_element_type=F32)
    acc = acc + jax.nn.sigmoid(gg_ref[...]) * jnp.dot(og_ref[...], wb_ref[2], preferred_element_type=F32)
    o_ref[...] = acc.astype(BF16)


def _merge(o_r, o_s, o_g, w_branch, z, d, tm, tn):
    n = o_r.shape[0]
    assert COL_GATES % tn == 0 and d % tn == 0
    branch = pl.BlockSpec((tm, WIDTH), lambda j, i: (i, 0))
    gate = lambda bidx: pl.BlockSpec((tm, tn), lambda j, i, bidx=bidx: (i, (COL_GATES + bidx * d) // tn + j))
    return pl.pallas_call(
        _merge_kernel,
        grid=(d // tn, n // tm),
        in_specs=[branch, branch, branch,
                  pl.BlockSpec((3, WIDTH, tn), lambda j, i: (0, 0, j)),
                  gate(0), gate(1), gate(2)],
        out_specs=pl.BlockSpec((tm, tn), lambda j, i: (i, j)),
        out_shape=jax.ShapeDtypeStruct((n, d), BF16),
        compiler_params=_params("parallel", "parallel"),
        name="branch_merge",
    )(o_r, o_s, o_g, w_branch, z, z, z)


def _norm_router_kernel(x_ref, g_ref, wr_ref, h_ref, rt_ref):
    h = _rms(x_ref[...]) * g_ref[...]
    h_ref[...] = h
    (h_hi, h_lo), (w_hi, w_lo) = _split2(h), _split2(wr_ref[...])
    d32 = lambda a, b: jnp.dot(a, b, preferred_element_type=F32)
    logits = d32(h_hi, w_hi) + (d32(h_hi, w_lo) + d32(h_lo, w_hi))
    lane = lax.broadcasted_iota(jnp.int32, logits.shape, 1)
    neg = jnp.float32(-jnp.inf)
    lg = jnp.where(lane < N_EXPERTS, logits, neg)
    m1 = jnp.max(lg, axis=-1, keepdims=True)
    i1 = jnp.min(jnp.where(lg == m1, lane, LANES), axis=-1, keepdims=True)
    lg2 = jnp.where(lane == i1, neg, lg)
    m2 = jnp.max(lg2, axis=-1, keepdims=True)
    i2 = jnp.min(jnp.where(lg2 == m2, lane, LANES), axis=-1, keepdims=True)
    e = jnp.exp(m2 - m1)
    g1 = 1.0 / (1.0 + e)
    g2 = e / (1.0 + e)
    rt_ref[...] = jnp.where(lane == 0, i1.astype(F32),
                            jnp.where(lane == 1, i2.astype(F32),
                                      jnp.where(lane == 2, g1, jnp.where(lane == 3, g2, 0.0))))


def _norm_router(x2, gain, w_router_pad, tm):
    n, d = x2.shape
    return pl.pallas_call(
        _norm_router_kernel,
        grid=(n // tm,),
        in_specs=[pl.BlockSpec((tm, d), lambda i: (i, 0)), pl.BlockSpec((1, d), lambda i: (0, 0)),
                  pl.BlockSpec((d, LANES), lambda i: (0, 0))],
        out_specs=[pl.BlockSpec((tm, d), lambda i: (i, 0)), pl.BlockSpec((tm, LANES), lambda i: (i, 0))],
        out_shape=[jax.ShapeDtypeStruct((n, d), F32), jax.ShapeDtypeStruct((n, LANES), F32)],
        compiler_params=_params("parallel"),
        name="norm_router",
    )(x2, gain, w_router_pad)


def _dense_ffn_kernel(x_ref, g_ref, wg_ref, wu_ref, wd_ref, o_ref, xb_ref):
    @pl.when(pl.program_id(1) == 0)
    def _():
        x = x_ref[...]
        xb_ref[...] = (_rms(x) * g_ref[...]).astype(BF16)
        o_ref[...] = x

    xb = xb_ref[...]
    gate = jnp.dot(xb, wg_ref[...], preferred_element_type=F32)
    up = jnp.dot(xb, wu_ref[...], preferred_element_type=F32)
    act = (_silu(gate) * up).astype(BF16)
    o_ref[...] += jnp.dot(act, wd_ref[...], preferred_element_type=F32)


def _dense_ffn(x2, norm_w, w_gate, w_up, w_down, bm, tf):
    n, d = x2.shape
    d_ff = w_gate.shape[1]
    rows = pl.BlockSpec((bm, d), lambda i, f: (i, 0))
    return pl.pallas_call(
        _dense_ffn_kernel,
        grid=(n // bm, d_ff // tf),
        in_specs=[rows, pl.BlockSpec((1, d), lambda i, f: (0, 0)),
                  pl.BlockSpec((d, tf), lambda i, f: (0, f)), pl.BlockSpec((d, tf), lambda i, f: (0, f)),
                  pl.BlockSpec((tf, d), lambda i, f: (f, 0))],
        out_specs=rows,
        out_shape=jax.ShapeDtypeStruct((n, d), F32),
        scratch_shapes=[pltpu.VMEM((bm, d), BF16)],
        compiler_params=_params("parallel", "arbitrary"),
        name="dense_ffn",
    )(x2, norm_w[None, :], w_gate.astype(BF16), w_up.astype(BF16), w_down.astype(BF16))


def _row_copy(src_hbm, dst_ref, src_row, dst_row, sem):
    return pltpu.make_async_copy(src_hbm.at[pl.ds(src_row, 1), :], dst_ref.at[pl.ds(dst_row, 1), :], sem)


def _moe_swiglu_kernel(be_ref, nu_ref, tok_ref, h_hbm, wg_ref, wu_ref, wd_ref, o_ref, xg_ref, xb_ref, sem_ref,
                       *, bm, nf):
    blk, f = pl.program_id(0), pl.program_id(1)
    n_used = nu_ref[0]
    slot = blk % 2
    rows_per_step = -(-bm // nf)
    fetched = rows_per_step * nf

    def fetch(block, row, to_slot):
        return _row_copy(h_hbm, xg_ref.at[to_slot], tok_ref[block * bm + row], row, sem_ref.at[to_slot])

    def wait_fetched(of_slot):
        def wait(r, carry):
            _row_copy(h_hbm, xg_ref.at[of_slot], 0, r, sem_ref.at[of_slot]).wait()
            return carry
        lax.fori_loop(0, fetched, wait, 0, unroll=8)

    @pl.when((blk == 0) & (f == 0))
    def _():
        def start(r, carry):
            fetch(0, r, 0).start()
            return carry
        lax.fori_loop(0, fetched, start, 0)

    @pl.when(blk < n_used)
    def _():
        @pl.when(f == 0)
        def _():
            wait_fetched(slot)
            xb_ref[...] = xg_ref[slot, 0:bm, :].astype(BF16)
            o_ref[...] = jnp.zeros_like(o_ref)

        for r in range(rows_per_step):
            fetch(blk + 1, f * rows_per_step + r, 1 - slot).start()

        xb = xb_ref[...]
        gate = jnp.dot(xb, wg_ref[0], preferred_element_type=F32)
        up = jnp.dot(xb, wu_ref[0], preferred_element_type=F32)
        act = (_silu(gate) * up).astype(BF16)
        o_ref[...] += jnp.dot(act, wd_ref[0], preferred_element_type=F32)

    @pl.when((blk >= n_used) & (f == 0))
    def _():
        @pl.when(blk == n_used)
        def _():
            wait_fetched(slot)
        o_ref[...] = jnp.zeros_like(o_ref)


def _moe_swiglu(h, slot_tok, w_gate, w_up, w_down, block_expert, n_used, bm, tf):
    d = h.shape[1]
    d_ff = w_gate.shape[2]
    nb, nf = slot_tok.shape[0] // bm, d_ff // tf
    buffer_rows = -(-(-(-bm // nf) * nf) // SUBLANES) * SUBLANES

    def f_idx(blk, f, nu):
        return jnp.where(blk < nu[0], f, nf - 1)

    def w_in_map(blk, f, be, nu, tok):
        return (be[jnp.minimum(blk, nu[0] - 1)], 0, f_idx(blk, f, nu))

    def w_out_map(blk, f, be, nu, tok):
        return (be[jnp.minimum(blk, nu[0] - 1)], f_idx(blk, f, nu), 0)

    return pl.pallas_call(
        functools.partial(_moe_swiglu_kernel, bm=bm, nf=nf),
        grid_spec=pltpu.PrefetchScalarGridSpec(
            num_scalar_prefetch=3, grid=(nb, nf),
            in_specs=[pl.BlockSpec(memory_space=pl.ANY),
                      pl.BlockSpec((1, d, tf), w_in_map), pl.BlockSpec((1, d, tf), w_in_map),
                      pl.BlockSpec((1, tf, d), w_out_map)],
            out_specs=pl.BlockSpec((bm, d), lambda blk, f, be, nu, tok: (blk, 0)),
            scratch_shapes=[pltpu.VMEM((2, buffer_rows, d), F32), pltpu.VMEM((bm, d), BF16),
                            pltpu.SemaphoreType.DMA((2,))]),
        out_shape=jax.ShapeDtypeStruct((nb * bm, d), F32),
        compiler_params=_params("arbitrary", "arbitrary"),
        name="moe_swiglu",
    )(block_expert, n_used, slot_tok, h, w_gate, w_up, w_down)


def _combine_kernel(pos_ref, x_ref, rt_ref, y_hbm, o_ref, b0_ref, b1_ref, sem, *, tm):
    base = pl.program_id(0) * tm

    def start(r, carry):
        _row_copy(y_hbm, b0_ref, pos_ref[2 * (base + r)], r, sem).start()
        _row_copy(y_hbm, b1_ref, pos_ref[2 * (base + r) + 1], r, sem).start()
        return carry

    def wait(r, carry):
        _row_copy(y_hbm, b0_ref, 0, r, sem).wait()
        _row_copy(y_hbm, b1_ref, 0, r, sem).wait()
        return carry

    lax.fori_loop(0, tm, start, 0, unroll=8)
    lax.fori_loop(0, tm, wait, 0, unroll=8)
    rt = rt_ref[...]
    o_ref[...] = x_ref[...] + (b0_ref[...] * rt[:, 2:3] + b1_ref[...] * rt[:, 3:4])


def _combine(x2, route, y_sorted, pos, tm):
    n, d = x2.shape
    return pl.pallas_call(
        functools.partial(_combine_kernel, tm=tm),
        grid_spec=pltpu.PrefetchScalarGridSpec(
            num_scalar_prefetch=1, grid=(n // tm,),
            in_specs=[pl.BlockSpec((tm, d), lambda i, p: (i, 0)),
                      pl.BlockSpec((tm, LANES), lambda i, p: (i, 0)),
                      pl.BlockSpec(memory_space=pl.ANY)],
            out_specs=pl.BlockSpec((tm, d), lambda i, p: (i, 0)),
            scratch_shapes=[pltpu.VMEM((tm, d), F32), pltpu.VMEM((tm, d), F32), pltpu.SemaphoreType.DMA(())]),
        out_shape=jax.ShapeDtypeStruct((n, d), F32),
        compiler_params=_params("arbitrary"),
        name="combine",
    )(pos, x2, route, y_sorted)


def _tile(n, want):
    for cand in range(min(n, want), 0, -LANES):
        if n % cand == 0:
            return cand
    raise ValueError((n, want))


def _mixer(x2, b, t, norm_w, w_in, sb_qn, sb_kn, conv_w, a_log, dt_bias, gdn_on, w_branch, w_out):
    n, d = x2.shape
    w_main = jnp.concatenate([w_in[:, :SMALL_OFF], w_in[:, SMALL_OFF + 2 * HEADS:]], axis=1).astype(BF16)
    w_small = w_in[:, SMALL_OFF:SMALL_OFF + 2 * HEADS]
    w_small_pad = jnp.pad(w_small, ((0, 0), (0, LANES - 2 * HEADS))).astype(BF16)
    hidden, zs, zst = _norm_small(x2, norm_w[None, :], w_small_pad, w_small.T.astype(BF16), _tile(n, 512))
    z = _matmul(hidden, w_main, _tile(n, 2048), _tile(w_main.shape[1], 1024), F32)
    o_r = _retention(z, b, t, _tile(t, 512))
    o_s = _stick_breaking(z, sb_qn[None, :], sb_kn[None, :], b, t)
    o_g = _gated_deltanet(z, zs, zst, conv_w, a_log, dt_bias, gdn_on[None, :], b, t)
    merged = _merge(o_r, o_s, o_g, w_branch.astype(BF16), z, d, _tile(n, 512), _tile(d, 1024))
    return _matmul_residual(merged, w_out.astype(BF16), x2, _tile(n, 1024), _tile(d, 1024))


def _moe_ffn(x2, norm_w, w_router, w_gate, w_up, w_down, bm, tf):
    n, d = x2.shape
    h, route = _norm_router(x2, norm_w[None, :], jnp.pad(w_router, ((0, 0), (0, LANES - N_EXPERTS))),
                            _tile(n, 512))
    flat_e = route[:, :TOP_K].astype(jnp.int32).reshape(-1)
    n_assign = n * TOP_K
    onehot = (flat_e[:, None] == jnp.arange(N_EXPERTS, dtype=jnp.int32)[None, :]).astype(jnp.int32)
    running = jnp.cumsum(onehot, axis=0)
    counts = running[-1]
    rank = jnp.sum((running - onehot) * onehot, axis=1)
    padded = (counts + bm - 1) // bm * bm
    pad_end = jnp.cumsum(padded)
    pad_start = pad_end - padded
    dest = (pad_start[flat_e] + rank).astype(jnp.int32)
    n_blocks = -(-n_assign // bm) + N_EXPERTS
    n_slots = n_blocks * bm
    flat_tok = jnp.arange(n_assign, dtype=jnp.int32) // TOP_K
    slot_tok = jnp.zeros((n_slots,), jnp.int32).at[dest].set(flat_tok)
    block_e = jnp.minimum(jnp.searchsorted(pad_end, jnp.arange(n_blocks, dtype=jnp.int32) * bm, side='right'),
                          N_EXPERTS - 1).astype(jnp.int32)
    n_used = (pad_end[-1:] // bm).astype(jnp.int32)
    y_sorted = _moe_swiglu(h, slot_tok, w_gate.astype(BF16), w_up.astype(BF16), w_down.astype(BF16),
                           block_e, n_used, bm, tf)
    return _combine(x2, route, y_sorted, dest, _tile(n, 256))


def kernel(x, mix_norm, w_in, sb_q_norm, sb_k_norm, gdn_conv, gdn_a_log, gdn_dt_bias, gdn_out_norm, w_branch,
           w_out, ffn_norm, w_ffn_gate, w_ffn_up, w_ffn_down, w_router, w_exp_gate, w_exp_up, w_exp_down):
    b, t, d = x.shape
    depth = mix_norm.shape[0]
    d_ff = w_ffn_gate.shape[2]
    x2 = x.reshape(b * t, d)
    bm = _tile(b * t, 512)
    tf = _tile(d_ff, 512)
    for layer in range(depth):
        x2 = _mixer(x2, b, t, mix_norm[layer], w_in[layer], sb_q_norm[layer], sb_k_norm[layer], gdn_conv[layer],
                    gdn_a_log[layer], gdn_dt_bias[layer], gdn_out_norm[layer], w_branch[layer], w_out[layer])
        i = layer // 2
        if layer % 2 == 0:
            x2 = _dense_ffn(x2, ffn_norm[layer], w_ffn_gate[i], w_ffn_up[i], w_ffn_down[i],
                            _tile(b * t, 2 * bm), tf)
        else:
            x2 = _moe_ffn(x2, ffn_norm[layer], w_router[i], w_exp_gate[i], w_exp_up[i], w_exp_down[i], bm, tf)
    return x2.reshape(b, t, d)
```
